```python
import math
import jax, jax.numpy as jnp
from jax import lax
import numpy as np

D_MODEL = 1024
BATCH = 8
SEQ = 2048
DEPTH = 4
DEC_BATCH = 32
DEC_SEQ = 4
PAST_LEN = 8192
PAGE_SIZE = 128

HEAD_DIM = 64
ROPE_THETA = 10000.0
EPS = 1e-6
Q_BLOCK = 128

GLA_HEADS = 4
GLA_DK = 32
GLA_DV = 64
GLA_RANK = 16
GLA_TAU = 16.0
GLA_CHUNK = 64

SSD_HEADS = 4
SSD_P = 64
SSD_GROUPS = 2
SSD_N = 128
SSD_CONV = 4
SSD_CHUNK = 64

MOBA_HEADS = 4
MOBA_BLOCK = 256
MOBA_TOPK = 3

DSA_HEADS = 4
DSA_IDX_HEADS = 8
DSA_IDX_DIM = 64
DSA_TOPK = 256

MEM_LEN = 256
XA_HEADS = 4
XA_DIM = D_MODEL // XA_HEADS

GLA_QK_W = GLA_HEADS * GLA_DK
GLA_V_W = GLA_HEADS * GLA_DV
SSD_INNER = SSD_HEADS * SSD_P
SSD_BC = SSD_GROUPS * SSD_N
SSD_CONV_DIM = SSD_INNER + 2 * SSD_BC
MOBA_W = MOBA_HEADS * HEAD_DIM
DSA_W = DSA_HEADS * HEAD_DIM
DSA_IDX_W = DSA_IDX_HEADS * DSA_IDX_DIM
MIX_W = GLA_V_W + SSD_INNER + MOBA_W + DSA_W
IN_SPLITS = (GLA_QK_W, GLA_QK_W, GLA_V_W, GLA_V_W, GLA_RANK,
             SSD_INNER, SSD_CONV_DIM, SSD_HEADS,
             MOBA_W, MOBA_W, MOBA_W, MOBA_W,
             DSA_W, DSA_W, DSA_W, DSA_W, DSA_IDX_W, DSA_IDX_DIM, DSA_IDX_HEADS)
IN_DIM = sum(IN_SPLITS)

kernel_name = 'hymba_gla_ssd_moba_dsa_step'


def _rmsnorm(x, g):
    xf = x.astype(jnp.float32)
    y = xf * lax.rsqrt(jnp.mean(xf * xf, axis=-1, keepdims=True) + EPS)
    return (y * g.astype(jnp.float32)).astype(x.dtype)


def _rope(x, pos):
    half = x.shape[-1] // 2
    inv = ROPE_THETA ** (-jnp.arange(half, dtype=jnp.float32) / half)
    ang = pos.astype(jnp.float32)[:, None] * inv[None, :]
    cos = jnp.cos(ang)[:, None, :]
    sin = jnp.sin(ang)[:, None, :]
    xf = x.astype(jnp.float32)
    x1, x2 = xf[..., :half], xf[..., half:]
    return jnp.concatenate([x1 * cos - x2 * sin, x2 * cos + x1 * sin], axis=-1).astype(x.dtype)


def _split_cols(h):
    return jnp.split(h, np.cumsum(IN_SPLITS)[:-1].tolist(), axis=-1)


def _to_chunks(t, C):
    B, L = t.shape[:2]
    t = t.reshape((B, L // C, C) + t.shape[2:])
    return jnp.moveaxis(jnp.moveaxis(t, 1, 0), 2, 3)


def _from_chunks(t):
    n, B, H, C = t.shape[:4]
    t = jnp.moveaxis(jnp.moveaxis(t, 3, 2), 0, 1)
    return t.reshape((B, n * C, H) + t.shape[4:])


def _gla_chunked(q, k, v, log_f, s0):
    L = q.shape[1]
    C = GLA_CHUNK if L % GLA_CHUNK == 0 else L
    causal = jnp.tril(jnp.ones((C, C), dtype=bool))

    def step(S, inp):
        qc, kc, vc, gc = inp
        G = jnp.cumsum(gc, axis=2)
        G_end = G[:, :, -1, :]
        q_dec = qc * jnp.exp(G)
        att = jnp.einsum('bhtd,bhsd->bhts', q_dec, kc * jnp.exp(-G))
        att = jnp.where(causal, att, 0.0)
        o = jnp.einsum('bhts,bhsv->bhtv', att, vc) + jnp.einsum('bhtd,bhdv->bhtv', q_dec, S)
        k_end = kc * jnp.exp(G_end[:, :, None, :] - G)
        S = S * jnp.exp(G_end)[..., None] + jnp.einsum('bhsd,bhsv->bhdv', k_end, vc)
        return S, o

    xs = tuple(_to_chunks(t.astype(jnp.float32), C) for t in (q, k, v, log_f))
    S, o = lax.scan(step, s0.astype(jnp.float32), xs)
    return _from_chunks(o), S


def _ssd_chunked(x, dt, a_neg, bm, cm, s0):
    L = x.shape[1]
    C = SSD_CHUNK if L % SSD_CHUNK == 0 else L
    causal = jnp.tril(jnp.ones((C, C), dtype=bool))

    def step(S, inp):
        xc, lac, bc, cc = inp
        cum = jnp.cumsum(lac, axis=-1)
        seg = jnp.where(causal, cum[..., :, None] - cum[..., None, :], -jnp.inf)
        w = jnp.einsum('bhtn,bhsn->bhts', cc, bc) * jnp.exp(seg)
        y = jnp.einsum('bhts,bhsp->bhtp', w, xc) + jnp.einsum('bhtn,bhpn->bhtp', cc, S) * jnp.exp(cum)[..., None]
        to_end = jnp.exp(cum[..., -1:] - cum)
        S = S * jnp.exp(cum[..., -1])[..., None, None] + jnp.einsum('bhsn,bhsp->bhpn', bc * to_end[..., None], xc)
        return S, y

    xs = (_to_chunks(x * dt[..., None], C), _to_chunks(dt * a_neg, C), _to_chunks(bm, C), _to_chunks(cm, C))
    S, y = lax.scan(step, s0.astype(jnp.float32), xs)
    return _from_chunks(y), S


def _map_qblocks(fn, xs):
    Lq = xs[0].shape[0]
    qb = Q_BLOCK if Lq % Q_BLOCK == 0 else Lq
    nq = Lq // qb
    out = lax.map(fn, tuple(t.reshape((nq, qb) + t.shape[1:]) for t in xs))
    return out.reshape((Lq,) + out.shape[2:])


def _moba_seq(q, k, v, q_pos):
    Lk, H, d = k.shape
    nb = -(-Lk // MOBA_BLOCK)
    pad = nb * MOBA_BLOCK - Lk
    kb = jnp.pad(k, ((0, pad), (0, 0), (0, 0))).reshape(nb, MOBA_BLOCK, H, d).transpose(2, 0, 1, 3).astype(jnp.float32)
    vb = jnp.pad(v, ((0, pad), (0, 0), (0, 0))).reshape(nb, MOBA_BLOCK, H, d).transpose(2, 0, 1, 3).astype(jnp.float32)
    kmean = jnp.mean(kb, axis=2)
    n_sel = min(MOBA_TOPK, nb)
    hidx = jnp.arange(H)[None, :, None]
    offs = jnp.arange(MOBA_BLOCK, dtype=jnp.int32)
    blk_ids = jnp.arange(nb, dtype=jnp.int32)

    def attend_block(args):
        qb, pb = args
        Qb = qb.shape[0]
        qf = qb.astype(jnp.float32)
        cur = pb // MOBA_BLOCK
        gate = jnp.einsum('qhd,hnd->qhn', qf, kmean)
        gate = jnp.where(blk_ids[None, None, :] < cur[:, None, None], gate, -jnp.inf)
        _, sel = lax.top_k(gate, n_sel)
        own = jnp.broadcast_to(cur[:, None, None], (Qb, H, 1))
        blocks = jnp.concatenate([sel, own], axis=-1)
        blk_ok = jnp.concatenate([sel < cur[:, None, None], jnp.ones((Qb, H, 1), dtype=bool)], axis=-1)
        kg = kb[hidx, blocks]
        vg = vb[hidx, blocks]
        kpos = blocks[..., None] * MOBA_BLOCK + offs
        mask = blk_ok[..., None] & (kpos <= pb[:, None, None, None])
        s = jnp.einsum('qhd,qhnkd->qhnk', qf, kg)
        s = jnp.where(mask, s, -jnp.inf).reshape(Qb, H, -1)
        p = jax.nn.softmax(s, axis=-1).reshape(Qb, H, n_sel + 1, MOBA_BLOCK)
        return jnp.einsum('qhnk,qhnkd->qhd', p, vg).astype(q.dtype)

    return _map_qblocks(attend_block, (q, q_pos))


def _moba(q, k, v, q_pos):
    return lax.map(lambda a: _moba_seq(a[0], a[1], a[2], q_pos), (q, k, v))


def _dsa_seq(q, qi, wi, ki, q_pos, fetch):
    Lk = ki.shape[0]
    n_sel = max(1, min(DSA_TOPK, Lk // 4))
    kpos = jnp.arange(Lk, dtype=jnp.int32)
    kif = ki.astype(jnp.float32)

    def attend_block(args):
        qb, qib, wib, pb = args
        rel = jax.nn.relu(jnp.einsum('qhd,sd->qhs', qib.astype(jnp.float32), kif))
        score = jnp.einsum('qh,qhs->qs', wib.astype(jnp.float32), rel)
        score = jnp.where(kpos[None, :] <= pb[:, None], score, -jnp.inf)
        _, sel = lax.top_k(score, n_sel)
        valid = sel <= pb[:, None]
        kg, vg = fetch(sel)
        s = jnp.einsum('qhd,qkhd->qhk', qb.astype(jnp.float32), kg.astype(jnp.float32))
        s = jnp.where(valid[:, None, :], s, -jnp.inf)
        p = jax.nn.softmax(s, axis=-1)
        return jnp.einsum('qhk,qkhd->qhd', p, vg.astype(jnp.float32)).astype(q.dtype)

    return _map_qblocks(attend_block, (q, qi, wi, q_pos))


def _dsa(q, qi, wi, ki_all, k_new, v_new, q_pos, pool):
    past_len = ki_all.shape[1] - k_new.shape[1]
    if pool is None:
        def body(a):
            q1, qi1, wi1, ki1, kn, vn = a
            return _dsa_seq(q1, qi1, wi1, ki1, q_pos, lambda sel: (kn[sel], vn[sel]))
        return lax.map(body, (q, qi, wi, ki_all, k_new, v_new))
    k_pool, v_pool, ptab = pool

    def body(a):
        q1, qi1, wi1, ki1, kn, vn, prow = a

        def fetch(sel):
            in_past = (sel < past_len)[..., None, None]
            sp = jnp.minimum(sel, past_len - 1)
            page = prow[sp // PAGE_SIZE]
            off = sp % PAGE_SIZE
            sn = jnp.clip(sel - past_len, 0, kn.shape[0] - 1)
            return (jnp.where(in_past, k_pool[page, off], kn[sn]), jnp.where(in_past, v_pool[page, off], vn[sn]))

        return _dsa_seq(q1, qi1, wi1, ki1, q_pos, fetch)

    return lax.map(body, (q, qi, wi, ki_all, k_new, v_new, ptab))


def _gather_pages(pool, ptab):
    g = pool[ptab]
    return g.reshape((ptab.shape[0], ptab.shape[1] * pool.shape[1]) + pool.shape[2:])


def _memory_kv(mem, lp):
    B, M, _ = mem.shape
    hm = _rmsnorm(mem, lp['norm_mem'])
    k = jnp.einsum('bmd,de->bme', hm, lp['xa_wk']).reshape(B, M, XA_HEADS, XA_DIM)
    v = jnp.einsum('bmd,de->bme', hm, lp['xa_wv']).reshape(B, M, XA_HEADS, XA_DIM)
    return k, v


def _cross_attend(h, mem_k, mem_v, wq, wo):
    B, L, _ = h.shape
    q = jnp.einsum('bld,de->ble', h, wq).reshape(B, L, XA_HEADS, XA_DIM) * XA_DIM ** -0.5
    s = jnp.einsum('blhd,bmhd->bhlm', q.astype(jnp.float32), mem_k.astype(jnp.float32))
    p = jax.nn.softmax(s, axis=-1)
    o = jnp.einsum('bhlm,bmhd->blhd', p, mem_v.astype(jnp.float32)).astype(h.dtype)
    return jnp.einsum('ble,ed->bld', o.reshape(B, L, XA_HEADS * XA_DIM), wo)


def _layer(x, pos, mem_k, mem_v, gla_s0, ssd_s0, conv0, moba_k_past, moba_v_past, dsa_ik_past, dsa_pool, lp):
    B, L, _ = x.shape
    f32 = jnp.float32
    h = _rmsnorm(x, lp['norm_mix_pre'])
    proj = jnp.einsum('bld,de->ble', h, lp['w_in'])
    (g_q, g_k, g_v, g_gate, g_lr, s_z, s_xbc, s_dt,
     m_q, m_k, m_v, m_gate, d_q, d_k, d_v, d_gate, d_qi, d_ki, d_wi) = _split_cols(proj)

    q = g_q.reshape(B, L, GLA_HEADS, GLA_DK) * GLA_DK ** -0.5
    k = g_k.reshape(B, L, GLA_HEADS, GLA_DK)
    v = g_v.reshape(B, L, GLA_HEADS, GLA_DV)
    gate_logit = jnp.einsum('blr,re->ble', g_lr, lp['gla_w_lr']) + lp['gla_b_lr']
    log_f = (jax.nn.log_sigmoid(gate_logit.astype(f32)) / GLA_TAU).reshape(B, L, GLA_HEADS, GLA_DK)
    o_gla, gla_s = _gla_chunked(q, k, v, log_f, gla_s0)
    o_gla = _rmsnorm(o_gla.astype(x.dtype), lp['gla_norm']).reshape(B, L, GLA_V_W) * jax.nn.silu(g_gate)

    buf = jnp.concatenate([conv0.astype(s_xbc.dtype), s_xbc], axis=1)
    conv = lp['ssd_conv_b'] + buf[:, 0:L] * lp['ssd_conv_w'][0]
    for j in range(1, SSD_CONV):
        conv = conv + buf[:, j:j + L] * lp['ssd_conv_w'][j]
    conv_new = buf[:, L:]
    xbc = jax.nn.silu(conv)
    s_x, s_b, s_c = jnp.split(xbc, [SSD_INNER, SSD_INNER + SSD_BC], axis=-1)
    rep = SSD_HEADS // SSD_GROUPS
    s_b = jnp.repeat(s_b.reshape(B, L, SSD_GROUPS, SSD_N), rep, axis=2).astype(f32)
    s_c = jnp.repeat(s_c.reshape(B, L, SSD_GROUPS, SSD_N), rep, axis=2).astype(f32)
    dt = jax.nn.softplus(s_dt.astype(f32) + lp['ssd_dt_bias'].astype(f32))
    a_neg = -jnp.exp(lp['ssd_a_log'].astype(f32))
    xh = s_x.reshape(B, L, SSD_HEADS, SSD_P).astype(f32)
    y, ssd_s = _ssd_chunked(xh, dt, a_neg, s_b, s_c, ssd_s0)
    y = y + xh * lp['ssd_d'].astype(f32)[:, None]
    o_ssd = _rmsnorm((y.reshape(B, L, SSD_INNER) * jax.nn.silu(s_z.astype(f32))).astype(x.dtype), lp['ssd_norm'])

    mq = _rope(m_q.reshape(B, L, MOBA_HEADS, HEAD_DIM), pos) * HEAD_DIM ** -0.5
    mk = _rope(m_k.reshape(B, L, MOBA_HEADS, HEAD_DIM), pos)
    mv = m_v.reshape(B, L, MOBA_HEADS, HEAD_DIM)
    o_moba = _moba(mq, jnp.concatenate([moba_k_past, mk], axis=1), jnp.concatenate([moba_v_past, mv], axis=1), pos)
    o_moba = o_moba.reshape(B, L, MOBA_W) * jax.nn.silu(m_gate)

    dq = _rope(d_q.reshape(B, L, DSA_HEADS, HEAD_DIM), pos) * HEAD_DIM ** -0.5
    dk = _rope(d_k.reshape(B, L, DSA_HEADS, HEAD_DIM), pos)
    dv = d_v.reshape(B, L, DSA_HEADS, HEAD_DIM)
    qi = _rope(d_qi.reshape(B, L, DSA_IDX_HEADS, DSA_IDX_DIM), pos) * DSA_IDX_DIM ** -0.5
    ki = _rope(d_ki.reshape(B, L, 1, DSA_IDX_DIM), pos)[:, :, 0]
    wi = d_wi * DSA_IDX_HEADS ** -0.5
    o_dsa = _dsa(dq, qi, wi, jnp.concatenate([dsa_ik_past, ki], axis=1), dk, dv, pos, dsa_pool)
    o_dsa = o_dsa.reshape(B, L, DSA_W) * jax.nn.silu(d_gate)

    mix = jnp.concatenate([o_gla, o_ssd, o_moba, o_dsa], axis=-1)
    x = x + _rmsnorm(jnp.einsum('ble,ed->bld', mix, lp['w_out']), lp['norm_mix_post'])
    xa = _cross_attend(_rmsnorm(x, lp['norm_xa_pre']), mem_k, mem_v, lp['xa_wq'], lp['xa_wo'])
    x = x + _rmsnorm(xa, lp['norm_xa_post'])
    new_state = (gla_s.astype(x.dtype), ssd_s.astype(x.dtype), conv_new, mk, mv, dk, dv, ki)
    return x, new_state


def setup_inputs(seed: int = 0) -> dict:
    key = jax.random.key(seed)
    keys = jax.random.split(key, 40)
    it = iter(range(40))
    f32 = jnp.float32

    def nrm(shape, scale=1.0):
        return jax.random.normal(keys[next(it)], shape, f32) * scale

    def gain(shape):
        return 1.0 + 0.05 * jax.random.normal(keys[next(it)], shape, f32)

    n_pages = PAST_LEN // PAGE_SIZE
    n_pool = (DEC_BATCH * n_pages * 5 + 3) // 4
    x_prompt = nrm((BATCH, SEQ, D_MODEL))
    x_sample = nrm((DEC_BATCH, DEC_SEQ, D_MODEL))
    state_gla = nrm((DEPTH, DEC_BATCH, GLA_HEADS, GLA_DK, GLA_DV), 4.0)
    state_ssd = nrm((DEPTH, DEC_BATCH, SSD_HEADS, SSD_P, SSD_N), 0.05)
    state_conv = nrm((DEPTH, DEC_BATCH, SSD_CONV - 1, SSD_CONV_DIM))
    cache_moba_k = nrm((DEPTH, n_pool, PAGE_SIZE, MOBA_HEADS, HEAD_DIM))
    cache_moba_v = nrm((DEPTH, n_pool, PAGE_SIZE, MOBA_HEADS, HEAD_DIM))
    cache_dsa_k = nrm((DEPTH, n_pool, PAGE_SIZE, DSA_HEADS, HEAD_DIM))
    cache_dsa_v = nrm((DEPTH, n_pool, PAGE_SIZE, DSA_HEADS, HEAD_DIM))
    cache_dsa_ik = nrm((DEPTH, n_pool, PAGE_SIZE, DSA_IDX_DIM))
    cache_mem_k = nrm((DEPTH, DEC_BATCH, MEM_LEN, XA_HEADS, XA_DIM))
    cache_mem_v = nrm((DEPTH, DEC_BATCH, MEM_LEN, XA_HEADS, XA_DIM))
    page_table = jax.random.permutation(keys[next(it)], n_pool)[:DEC_BATCH * n_pages].reshape(DEC_BATCH, n_pages).astype(jnp.int32)
    mem_prompt = nrm((BATCH, MEM_LEN, D_MODEL))
    norm_mix_pre = gain((DEPTH, D_MODEL))
    norm_mix_post = gain((DEPTH, D_MODEL))
    norm_xa_pre = gain((DEPTH, D_MODEL))
    norm_xa_post = gain((DEPTH, D_MODEL))
    norm_mem = gain((DEPTH, D_MODEL))
    w_in = nrm((DEPTH, D_MODEL, IN_DIM), D_MODEL ** -0.5)
    gla_w_lr = nrm((DEPTH, GLA_RANK, GLA_QK_W), GLA_RANK ** -0.5)
    gla_b_lr = nrm((DEPTH, GLA_QK_W), 0.1)
    gla_norm = gain((DEPTH, GLA_DV))
    ssd_conv_w = nrm((DEPTH, SSD_CONV, SSD_CONV_DIM), SSD_CONV ** -0.5)
    ssd_conv_b = nrm((DEPTH, SSD_CONV_DIM), 0.02)
    dt0 = jnp.exp(jax.random.uniform(keys[next(it)], (DEPTH, SSD_HEADS), f32, math.log(1e-3), math.log(1e-1)))
    ssd_dt_bias = dt0 + jnp.log(-jnp.expm1(-dt0))
    ssd_a_log = jnp.log(jax.random.uniform(keys[next(it)], (DEPTH, SSD_HEADS), f32, 1.0, 16.0))
    ssd_d = gain((DEPTH, SSD_HEADS))
    ssd_norm = gain((DEPTH, SSD_INNER))
    w_out = nrm((DEPTH, MIX_W, D_MODEL), MIX_W ** -0.5)
    xa_wq = nrm((DEPTH, D_MODEL, XA_HEADS * XA_DIM), D_MODEL ** -0.5)
    xa_wk = nrm((DEPTH, D_MODEL, XA_HEADS * XA_DIM), D_MODEL ** -0.5)
    xa_wv = nrm((DEPTH, D_MODEL, XA_HEADS * XA_DIM), D_MODEL ** -0.5)
    xa_wo = nrm((DEPTH, XA_HEADS * XA_DIM, D_MODEL), (XA_HEADS * XA_DIM) ** -0.5)
    return {'x_prompt': x_prompt, 'x_sample': x_sample, 'state_gla': state_gla, 'state_ssd': state_ssd,
            'state_conv': state_conv, 'cache_moba_k': cache_moba_k, 'cache_moba_v': cache_moba_v,
            'cache_dsa_k': cache_dsa_k, 'cache_dsa_v': cache_dsa_v, 'cache_dsa_ik': cache_dsa_ik,
            'cache_mem_k': cache_mem_k, 'cache_mem_v': cache_mem_v, 'page_table': page_table,
            'mem_prompt': mem_prompt, 'norm_mix_pre': norm_mix_pre, 'norm_mix_post': norm_mix_post,
            'norm_xa_pre': norm_xa_pre, 'norm_xa_post': norm_xa_post, 'norm_mem': norm_mem, 'w_in': w_in,
            'gla_w_lr': gla_w_lr, 'gla_b_lr': gla_b_lr, 'gla_norm': gla_norm, 'ssd_conv_w': ssd_conv_w,
            'ssd_conv_b': ssd_conv_b, 'ssd_dt_bias': ssd_dt_bias, 'ssd_a_log': ssd_a_log, 'ssd_d': ssd_d,
            'ssd_norm': ssd_norm, 'w_out': w_out, 'xa_wq': xa_wq, 'xa_wk': xa_wk, 'xa_wv': xa_wv, 'xa_wo': xa_wo}


def reference(x_prompt, x_sample, state_gla, state_ssd, state_conv, cache_moba_k, cache_moba_v,
              cache_dsa_k, cache_dsa_v, cache_dsa_ik, cache_mem_k, cache_mem_v, page_table, mem_prompt,
              norm_mix_pre, norm_mix_post, norm_xa_pre, norm_xa_post, norm_mem, w_in, gla_w_lr, gla_b_lr,
              gla_norm, ssd_conv_w, ssd_conv_b, ssd_dt_bias, ssd_a_log, ssd_d, ssd_norm, w_out,
              xa_wq, xa_wk, xa_wv, xa_wo):
    weights = dict(norm_mix_pre=norm_mix_pre, norm_mix_post=norm_mix_post, norm_xa_pre=norm_xa_pre,
                   norm_xa_post=norm_xa_post, norm_mem=norm_mem, w_in=w_in, gla_w_lr=gla_w_lr,
                   gla_b_lr=gla_b_lr, gla_norm=gla_norm, ssd_conv_w=ssd_conv_w, ssd_conv_b=ssd_conv_b,
                   ssd_dt_bias=ssd_dt_bias, ssd_a_log=ssd_a_log, ssd_d=ssd_d, ssd_norm=ssd_norm, w_out=w_out,
                   xa_wq=xa_wq, xa_wk=xa_wk, xa_wv=xa_wv, xa_wo=xa_wo)

    dtype = x_prompt.dtype
    Bp, S, _ = x_prompt.shape
    pos_p = jnp.arange(S, dtype=jnp.int32)
    zero_gla = jnp.zeros((Bp, GLA_HEADS, GLA_DK, GLA_DV), dtype)
    zero_ssd = jnp.zeros((Bp, SSD_HEADS, SSD_P, SSD_N), dtype)
    zero_conv = jnp.zeros((Bp, SSD_CONV - 1, SSD_CONV_DIM), dtype)
    empty_kv = jnp.zeros((Bp, 0, MOBA_HEADS, HEAD_DIM), dtype)
    empty_ik = jnp.zeros((Bp, 0, DSA_IDX_DIM), dtype)
    x = x_prompt
    prompt_new = []
    for l in range(DEPTH):
        lp = {name: w[l] for name, w in weights.items()}
        mem_k, mem_v = _memory_kv(mem_prompt, lp)
        x, new = _layer(x, pos_p, mem_k, mem_v, zero_gla, zero_ssd, zero_conv,
                        empty_kv, empty_kv, empty_ik, None, lp)
        prompt_new.append(new + (mem_k, mem_v))
    y_prompt = x
    (p_gla, p_ssd, p_conv, p_moba_k, p_moba_v, p_dsa_k, p_dsa_v, p_dsa_ik, p_mem_k, p_mem_v) = [
        jnp.stack(t) for t in zip(*prompt_new)]

    past_len = page_table.shape[1] * cache_moba_k.shape[2]
    pos_s = past_len + jnp.arange(x_sample.shape[1], dtype=jnp.int32)
    x = x_sample
    sample_new = []
    for l in range(DEPTH):
        lp = {name: w[l] for name, w in weights.items()}
        x, new = _layer(x, pos_s, cache_mem_k[l], cache_mem_v[l], state_gla[l], state_ssd[l], state_conv[l],
                        _gather_pages(cache_moba_k[l], page_table), _gather_pages(cache_moba_v[l], page_table),
                        _gather_pages(cache_dsa_ik[l], page_table), (cache_dsa_k[l], cache_dsa_v[l], page_table), lp)
        sample_new.append(new)
    y_sample = x
    (s_gla, s_ssd, s_conv, s_moba_k, s_moba_v, s_dsa_k, s_dsa_v, s_dsa_ik) = [
        jnp.stack(t) for t in zip(*sample_new)]

    return (y_prompt, y_sample, p_gla, p_ssd, p_conv, p_moba_k, p_moba_v, p_dsa_k, p_dsa_v, p_dsa_ik,
            p_mem_k, p_mem_v, s_gla, s_ssd, s_conv, s_moba_k, s_moba_v, s_dsa_k, s_dsa_v, s_dsa_ik)
```

```python
import functools

import jax
import jax.numpy as jnp
import numpy as np
from jax import lax
from jax.experimental import pallas as pl
from jax.experimental.pallas import tpu as pltpu

F32 = jnp.float32
BF16 = jnp.bfloat16
HI = lax.Precision.HIGHEST

D_MODEL = 1024
HEAD_DIM = 64
ROPE_THETA = 10000.0
EPS = 1e-6
N_HEADS = 4
GLA_DK = 32
GLA_RANK = 16
GLA_TAU = 16.0
SCAN_CHUNK = 64
SSD_N = 128
SSD_CONV = 4
MOBA_BLOCK = 256
MOBA_TOPK = 3
DSA_IDX_HEADS = 8
DSA_TOPK = 256
XA_DIM = 256
IN_SPLITS = (128, 128, 256, 256, 16, 256, 768, 4, 256, 256, 256, 256, 256, 256, 256, 256, 512, 64, 8)

LANES = 128
SUBLANES = 8
SAMPLE_PAD = SUBLANES
VMEM_LIMIT = 56 * 1024 * 1024
NEG = -1e30


def _nn(a, b, precision=None):
    return jnp.dot(a, b, preferred_element_type=F32, precision=precision)


def _nt(a, b, precision=None):
    return lax.dot_general(a, b, (((1,), (1,)), ((), ())), preferred_element_type=F32, precision=precision)


def _tn(a, b, precision=None):
    return lax.dot_general(a, b, (((0,), (0,)), ((), ())), preferred_element_type=F32, precision=precision)


def _b(x):
    return x.astype(BF16)


def _sigmoid(x):
    return 1.0 / (1.0 + jnp.exp(-x))


def _silu(x):
    return x * _sigmoid(x)


def _softplus(x):
    return jnp.maximum(x, 0.0) + jnp.log1p(jnp.exp(-jnp.abs(x)))


def _log_sigmoid(x):
    return jnp.minimum(x, 0.0) - jnp.log1p(jnp.exp(-jnp.abs(x)))


def _rms(x, g):
    return x * lax.rsqrt(jnp.mean(x * x, axis=-1, keepdims=True) + EPS) * g


def _iota(shape, dim):
    return lax.broadcasted_iota(jnp.int32, shape, dim)


def _params(n_axes):
    return pltpu.CompilerParams(dimension_semantics=("arbitrary",) * n_axes, vmem_limit_bytes=VMEM_LIMIT)


def _norm_proj_body(*refs, segs, has_rope):
    if has_rope:
        x_ref, g_ref, w_ref, cs_ref, sn_ref, *outs = refs
    else:
        x_ref, g_ref, w_ref, *outs = refs
    hb = _b(_rms(x_ref[...], g_ref[...]))
    for (c0, w, rope, scale, ow), o_ref in zip(segs, outs):
        y = _nn(hb, w_ref[:, c0:c0 + w])
        if rope:
            lane = _iota(y.shape, 1)
            sw = jnp.where((lane & 63) < 32, pltpu.roll(y, w - 32, 1), pltpu.roll(y, 32, 1))
            y = y * cs_ref[:, :w] + sw * sn_ref[:, :w]
        if scale != 1.0:
            y = y * scale
        o_ref[...] = y[:, :ow]


def _norm_proj(x, gain, w_cat, segs, rope_tabs=None, *, name):
    n, d = x.shape
    tm = min(256, n)
    has_rope = rope_tabs is not None
    in_specs = [pl.BlockSpec((tm, d), lambda i: (i, 0)),
                pl.BlockSpec((1, d), lambda i: (0, 0)),
                pl.BlockSpec(w_cat.shape, lambda i: (0, 0))]
    args = [x, gain.reshape(1, d), w_cat]
    if has_rope:
        cs, sn = rope_tabs
        nt = cs.shape[0] // tm
        in_specs += [pl.BlockSpec((tm, cs.shape[1]), lambda i: (i % nt, 0))] * 2
        args += [cs, sn]
    return pl.pallas_call(
        functools.partial(_norm_proj_body, segs=segs, has_rope=has_rope),
        grid=(n // tm,),
        in_specs=in_specs,
        out_specs=[pl.BlockSpec((tm, s[4]), lambda i: (i, 0)) for s in segs],
        out_shape=[jax.ShapeDtypeStruct((n, s[4]), F32) for s in segs],
        compiler_params=_params(1),
        name=name,
    )(*args)


def _gla_body(q_ref, k_ref, v_ref, g_ref, lr_ref, wlr_ref, blr_ref, gn_ref, s0_ref, o_ref, so_ref, st_ref,
              *, T, C, valid, nt):
    t = pl.program_id(1)

    @pl.when(t == 0)
    def _():
        st_ref[...] = s0_ref[...]

    row = _iota((C, C), 0)
    col = _iota((C, C), 1)
    causal = col <= row
    tril = causal.astype(F32)
    lane_qk = _iota((1, 128), 1)
    lane_v = _iota((1, 256), 1)
    bd_state = (_iota((256, 128), 0) >> 6) == (_iota((256, 128), 1) >> 5)
    bd_head = ((_iota((256, 256), 0) >> 6) == (_iota((256, 256), 1) >> 6)).astype(F32)
    st = st_ref[...]
    for c in range(T // C):
        sl = slice(c * C, (c + 1) * C)
        q = q_ref[sl, :]
        k = k_ref[sl, :]
        v = v_ref[sl, :]
        logit = _nn(lr_ref[sl, :], wlr_ref[...], HI) + blr_ref[...]
        logf = _log_sigmoid(logit) * (1.0 / GLA_TAU)
        if valid < T:
            live = _iota((C, 1), 0) < valid
            logf = jnp.where(live, logf, 0.0)
            k = jnp.where(live, k, 0.0)
        gcum = _nn(tril, logf, HI)
        gend = gcum[C - 1:C, :]
        qd = q * jnp.exp(gcum)
        kinv = k * jnp.exp(-gcum)
        kend = k * jnp.exp(gend - gcum)
        o = _nt(_b(qd), _b(st))
        for h in range(N_HEADS):
            att = _nt(_b(jnp.where((lane_qk >> 5) == h, qd, 0.0)), _b(kinv))
            att = jnp.where(causal, att, 0.0)
            o = o + _nn(_b(att), _b(jnp.where((lane_v >> 6) == h, v, 0.0)))
        st = st * jnp.exp(gend) + jnp.where(bd_state, _tn(_b(v), _b(kend)), 0.0)
        ms = _nn(o * o, bd_head, HI) * (1.0 / HEAD_DIM)
        o_ref[sl, :] = o * lax.rsqrt(ms + EPS) * gn_ref[...] * _silu(g_ref[sl, :])
    st_ref[...] = st

    @pl.when(t == nt - 1)
    def _():
        so_ref[...] = st


def _gla(q, k, v, g, lr, wlr, blr, gn, s0t, *, B, L, valid, name):
    T = min(256, L)
    C = min(SCAN_CHUNK, T)
    nt = L // T
    assert valid == L or nt == 1
    tok = lambda w: pl.BlockSpec((T, w), lambda b, t: (b * nt + t, 0))
    const = lambda a: pl.BlockSpec(a.shape, lambda b, t: (0,) * a.ndim)
    st_spec = pl.BlockSpec((None, 256, 128), lambda b, t: (b, 0, 0))
    return pl.pallas_call(
        functools.partial(_gla_body, T=T, C=C, valid=valid, nt=nt),
        grid=(B, nt),
        in_specs=[tok(128), tok(128), tok(256), tok(256), tok(128), const(wlr), const(blr), const(gn), st_spec],
        out_specs=[tok(256), st_spec],
        out_shape=[jax.ShapeDtypeStruct((B * L, 256), F32), jax.ShapeDtypeStruct((B, 256, 128), F32)],
        scratch_shapes=[pltpu.VMEM((256, 128), F32)],
        compiler_params=_params(2),
        name=name,
    )(q, k, v, g, lr, wlr, blr, gn, s0t)


def _ssd_body(z_ref, xbc_ref, dtr_ref, c0_ref, e_ref, cw_ref, cb_ref, dtb_ref, alog_ref, d_ref, nw_ref, s0_ref,
              o_ref, so_ref, cn_ref, s_ref, xb_ref, *, T, C, valid, nt):
    t = pl.program_id(1)

    @pl.when(t == 0)
    def _():
        s_ref[...] = s0_ref[...]
        xb_ref[0:8, :] = c0_ref[...]

    xb_ref[8:8 + T, :] = xbc_ref[...]
    conv = cb_ref[...] + xb_ref[5:5 + T, :] * cw_ref[0:1, :]
    for j in range(1, SSD_CONV):
        conv = conv + xb_ref[5 + j:5 + j + T, :] * cw_ref[j:j + 1, :]
    act = _silu(conv)
    dt = _softplus(_nn(dtr_ref[...], e_ref[...], HI) + dtb_ref[...])
    v_last = valid - (nt - 1) * T
    if v_last < T:
        dt = jnp.where(_iota((T, 1), 0) < v_last, dt, 0.0)
    lac = dt * (-jnp.exp(alog_ref[...]))

    row = _iota((C, C), 0)
    col = _iota((C, C), 1)
    causal = col <= row
    tril = causal.astype(F32)
    triu = (row <= col).astype(F32)
    lane_x = _iota((1, 256), 1)
    s_all = s_ref[...]
    for c in range(T // C):
        sl = slice(c * C, (c + 1) * C)
        x = act[sl, 0:256]
        bm = act[sl, 256:512]
        cm = act[sl, 512:768]
        xdt = x * dt[sl, 0:256]
        cum64 = _nn(tril, lac[sl, 0:256], HI)
        cum128 = _nn(tril, lac[sl, 256:768], HI)
        cum_t = _tn(lac[sl, 256:768], triu, HI)
        cb = [_nt(_b(cm[:, g * 128:(g + 1) * 128]), _b(bm[:, g * 128:(g + 1) * 128])) for g in range(2)]
        y = x * d_ref[...]
        for h in range(N_HEADS):
            seg = cum128[:, h * 128:h * 128 + C] - cum_t[h * 128:h * 128 + C, :]
            w = jnp.where(causal, cb[h // 2] * jnp.exp(jnp.where(causal, seg, 0.0)), 0.0)
            y = y + _nn(_b(w), _b(jnp.where((lane_x >> 6) == h, xdt, 0.0)))
        ys0 = _nt(_b(cm[:, 0:128]), _b(s_all))
        ys1 = _nt(_b(cm[:, 128:256]), _b(s_all))
        y = y + jnp.where(lane_x < 128, ys0, ys1) * jnp.exp(cum64)
        new_rows = []
        for h in range(N_HEADS):
            g = h // 2
            c_end = cum128[C - 1:C, h * 128:(h + 1) * 128]
            bw = bm[:, g * 128:(g + 1) * 128] * jnp.exp(c_end - cum128[:, h * 128:(h + 1) * 128])
            upd = _tn(_b(xdt[:, h * 64:(h + 1) * 64]), _b(bw))
            new_rows.append(s_all[h * 64:(h + 1) * 64, :] * jnp.exp(c_end) + upd)
        s_all = jnp.concatenate(new_rows, axis=0)
        o_ref[sl, :] = _rms(y * _silu(z_ref[sl, :]), nw_ref[...])
    s_ref[...] = s_all
    xb_ref[0:8, :] = xb_ref[T:T + 8, :]

    @pl.when(t == nt - 1)
    def _():
        so_ref[...] = s_all
        cn_ref[...] = jnp.zeros_like(cn_ref)
        cn_ref[0:SSD_CONV - 1, :] = xb_ref[8 + v_last - (SSD_CONV - 1):8 + v_last, :]


def _ssd(z, xbc, dtr, conv0, consts, s0, s0_index, *, B, L, valid, name):
    T = min(256, L)
    C = min(SCAN_CHUNK, T)
    nt = L // T
    assert valid == L or nt == 1
    tok = lambda w: pl.BlockSpec((T, w), lambda b, t: (b * nt + t, 0))
    const = lambda a: pl.BlockSpec(a.shape, lambda b, t: (0,) * a.ndim)
    l_idx = s0_index
    s0_spec = pl.BlockSpec((None, None, 256, 128), lambda b, t: (l_idx, b, 0, 0))
    so_spec = pl.BlockSpec((None, 256, 128), lambda b, t: (b, 0, 0))
    cv_spec = pl.BlockSpec((None, 8, 768), lambda b, t: (b, 0, 0))
    return pl.pallas_call(
        functools.partial(_ssd_body, T=T, C=C, valid=valid, nt=nt),
        grid=(B, nt),
        in_specs=[tok(256), tok(768), tok(128), cv_spec] + [const(a) for a in consts] + [s0_spec],
        out_specs=[tok(256), so_spec, cv_spec],
        out_shape=[jax.ShapeDtypeStruct((B * L, 256), F32), jax.ShapeDtypeStruct((B, 256, 128), F32),
                   jax.ShapeDtypeStruct((B, 8, 768), F32)],
        scratch_shapes=[pltpu.VMEM((256, 128), F32), pltpu.VMEM((T + 8, 768), F32)],
        compiler_params=_params(2),
        name=name,
    )(z, xbc, dtr, conv0, *consts, s0)


def _masked_heads_attention(q, kb, vb, bias_of_head):
    lane = _iota((1, 256), 1)
    out = jnp.zeros(q.shape, F32)
    for h in range(N_HEADS):
        hm = (lane >> 6) == h
        s = _nt(_b(jnp.where(hm, q, 0.0)), kb) + bias_of_head(h)
        m = jnp.max(s, axis=-1, keepdims=True)
        p = jnp.exp(s - m)
        inv = 1.0 / jnp.sum(p, axis=-1, keepdims=True)
        out = out + jnp.where(hm, _nn(_b(p), vb) * inv, 0.0)
    return out


def _moba_prompt_body(q_ref, k_ref, v_ref, g_ref, o_ref, kb_ref, vb_ref, km_ref, *, L, nb):
    i = pl.program_id(1)
    T = MOBA_BLOCK

    @pl.when(i == 0)
    def _():
        kb_ref[...] = _b(k_ref[...])
        vb_ref[...] = _b(v_ref[...])
        rows = [jnp.sum(k_ref[n * T:(n + 1) * T, :], axis=0, keepdims=True) * (1.0 / T) for n in range(nb)]
        rows += [jnp.zeros((1, 256), F32)] * (8 - nb % 8 if nb % 8 else 0)
        km_ref[...] = jnp.concatenate(rows, axis=0)

    q = q_ref[...]
    nbp = km_ref.shape[0]
    km = km_ref[...]
    lane = _iota((1, 256), 1)
    blk = _iota((nbp, T), 0)
    key_blk = _iota((nbp, L), 1) >> 8
    expand = jnp.where(key_blk == _iota((nbp, L), 0), 1.0, 0.0).astype(BF16)
    qpos = i * T + _iota((T, L), 0)
    kpos = _iota((T, L), 1)
    own = (kpos >> 8) == i
    own_bias = jnp.where(kpos <= qpos, 0.0, NEG)
    n_sel = min(MOBA_TOPK, nb)

    def bias_of_head(h):
        gt = _nt(jnp.where((lane >> 6) == h, km, 0.0), q, HI)
        rank = jnp.zeros((nbp, T), jnp.int32)
        for n2 in range(nb):
            other = gt[n2:n2 + 1, :]
            beats = (other > gt) | ((other == gt) & (n2 < blk))
            rank = rank + jnp.where(beats & (n2 < i), 1, 0)
        sel_t = jnp.where((blk < i) & (rank < n_sel), 1.0, 0.0).astype(BF16)
        picked = _tn(sel_t, expand) > 0.5
        return jnp.where(own, own_bias, jnp.where(picked, 0.0, NEG))

    out = _masked_heads_attention(q, kb_ref[...], vb_ref[...], bias_of_head)
    o_ref[...] = out * _silu(g_ref[...])


def _moba_prompt(q, k, v, g, *, B, L, name):
    T = MOBA_BLOCK
    nq = L // T
    nb = L // T
    nbp = -(-nb // 8) * 8
    tok = pl.BlockSpec((T, 256), lambda b, i: (b * nq + i, 0))
    seq = pl.BlockSpec((L, 256), lambda b, i: (b, 0))
    return pl.pallas_call(
        functools.partial(_moba_prompt_body, L=L, nb=nb),
        grid=(B, nq),
        in_specs=[tok, seq, seq, tok],
        out_specs=tok,
        out_shape=jax.ShapeDtypeStruct((B * L, 256), F32),
        scratch_shapes=[pltpu.VMEM((L, 256), BF16), pltpu.VMEM((L, 256), BF16), pltpu.VMEM((nbp, 256), F32)],
        compiler_params=_params(2),
        name=name,
    )(q, k, v, g)


_INT_MIN = -2 ** 31


def _sortable_keys(score):
    bits = pltpu.bitcast(score, jnp.int32)
    keys = jnp.where(bits < 0, bits ^ jnp.int32(0x7FFFFFFF), bits)
    return jnp.where(score == 0.0, 0, keys)


def _kth_largest(keys, k, count):
    zero = count(jnp.zeros(keys.shape, jnp.int32))

    def body(i, ans):
        cand = ans | jnp.left_shift(jnp.int32(1), 31 - i)
        cnt = count(jnp.where(keys >= (cand ^ jnp.int32(_INT_MIN)), 1, 0))
        return jnp.where(cnt >= k, cand, ans)

    return lax.fori_loop(0, 32, body, zero) ^ jnp.int32(_INT_MIN)


def _dsa_prompt_body(q_ref, k_ref, v_ref, g_ref, qi_ref, wi_ref, ki_ref, o_ref,
                     kb_ref, vb_ref, kib_ref, bias_ref, keys_ref, *, L, T, n_sel):
    i = pl.program_id(1)

    @pl.when(i == 0)
    def _():
        kb_ref[...] = _b(k_ref[...])
        vb_ref[...] = _b(v_ref[...])
        kib_ref[...] = _b(ki_ref[...])

    kib = kib_ref[...]
    score = jnp.zeros((T, L), F32)
    for h in range(DSA_IDX_HEADS):
        rel = jnp.maximum(_nt(_b(qi_ref[:, h * 64:(h + 1) * 64]), kib), 0.0)
        score = score + wi_ref[:, h:h + 1] * rel
    qpos = i * T + _iota((T, L), 0)
    kpos = _iota((T, L), 1)
    causal = kpos <= qpos
    keys = _sortable_keys(jnp.where(causal, score, -jnp.inf))
    count = lambda m: jnp.sum(m, axis=1, keepdims=True)
    thr = _kth_largest(keys, n_sel, count)
    cnt_ge = count(jnp.where(keys >= thr, 1, 0))
    cnt_gt = count(jnp.where(keys > thr, 1, 0))
    few = (i * T + _iota((T, 1), 0) + 1) <= n_sel
    causal_bias = jnp.where(causal, 0.0, NEG)
    bias_ref[...] = jnp.where(few, causal_bias, jnp.where(keys >= thr, 0.0, NEG))
    tied = jnp.where(few, 0, jnp.where(cnt_ge > n_sel, 1, 0))
    keys_ref[...] = keys

    @pl.when(jnp.max(tied) > 0)
    def _():
        need = (n_sel - cnt_gt).astype(F32)
        tri = jnp.where(_iota((LANES, LANES), 0) <= _iota((LANES, LANES), 1), 1.0, 0.0).astype(BF16)
        seen = jnp.zeros((T, 1), F32)
        for c in range(L // LANES):
            cs = slice(c * LANES, (c + 1) * LANES)
            kc = keys_ref[:, cs]
            eq = jnp.where(kc == thr, 1.0, 0.0)
            upto = _nn(_b(eq), tri) + seen
            pick = jnp.where(kc > thr, 1.0, jnp.where(upto <= need, eq, 0.0))
            tie_bias = jnp.where(pick > 0.5, 0.0, NEG)
            ok = (c * LANES + _iota((T, LANES), 1)) <= (i * T + _iota((T, LANES), 0))
            bias_ref[:, cs] = jnp.where(few, jnp.where(ok, 0.0, NEG), tie_bias)
            seen = seen + jnp.sum(eq, axis=1, keepdims=True)

    out = _masked_heads_attention(q_ref[...], kb_ref[...], vb_ref[...], lambda h: bias_ref[...])
    o_ref[...] = out * _silu(g_ref[...])


def _dsa_prompt(q, k, v, g, qi, wi, ki, *, B, L, name):
    T = min(256, L)
    nq = L // T
    n_sel = max(1, min(DSA_TOPK, L // 4))
    tok = lambda w: pl.BlockSpec((T, w), lambda b, i: (b * nq + i, 0))
    seq = lambda w: pl.BlockSpec((L, w), lambda b, i: (b, 0))
    return pl.pallas_call(
        functools.partial(_dsa_prompt_body, L=L, T=T, n_sel=n_sel),
        grid=(B, nq),
        in_specs=[tok(256), seq(256), seq(256), tok(256), tok(512), tok(128), seq(64)],
        out_specs=tok(256),
        out_shape=jax.ShapeDtypeStruct((B * L, 256), F32),
        scratch_shapes=[pltpu.VMEM((L, 256), BF16), pltpu.VMEM((L, 256), BF16), pltpu.VMEM((L, 64), BF16),
                        pltpu.VMEM((T, L), F32), pltpu.VMEM((T, L), jnp.int32)],
        compiler_params=_params(2),
        name=name,
    )(q, k, v, g, qi, wi, ki)


def _stack_heads(q):
    lane = _iota((1, 256), 1)
    return jnp.concatenate([jnp.where((lane >> 6) == h, q, 0.0) for h in range(N_HEADS)], axis=0)


def _unstack_heads(o):
    lane = _iota((1, 256), 1)
    out = jnp.zeros((SAMPLE_PAD, 256), F32)
    for h in range(N_HEADS):
        out = out + jnp.where((lane >> 6) == h, o[h * SAMPLE_PAD:(h + 1) * SAMPLE_PAD, :], 0.0)
    return out


def _moba_sample_body(pt_ref, *refs, bps, nblk):
    kp = refs[0:2 * bps]
    vp = refs[2 * bps:4 * bps]
    q_ref, kn_ref, vn_ref, g_ref, o_ref, qs_ref, gate_ref, m_ref, l_ref, acc_ref = refs[4 * bps:]
    j = pl.program_id(1)
    R = N_HEADS * SAMPLE_PAD

    @pl.when(j == 0)
    def _():
        qs_ref[...] = _b(_stack_heads(q_ref[...]))
        gate_ref[...] = jnp.zeros_like(gate_ref)
        m_ref[...] = jnp.zeros_like(m_ref)
        l_ref[...] = jnp.zeros_like(l_ref)

    qs = qs_ref[...]
    lane = _iota((R, LANES), 1)
    for r in range(bps):
        n = j * bps + r
        kk = _b(jnp.concatenate([kp[2 * r][...], kp[2 * r + 1][...]], axis=0))
        vv = _b(jnp.concatenate([vp[2 * r][...], vp[2 * r + 1][...]], axis=0))
        s = _nt(qs, kk)
        m = jnp.max(s, axis=-1, keepdims=True)
        p = jnp.exp(s - m)
        here = (lane & (nblk - 1)) == n
        gate_ref[...] = jnp.where(here, jnp.sum(s, axis=-1, keepdims=True) * (1.0 / MOBA_BLOCK), gate_ref[...])
        m_ref[...] = jnp.where(here, m, m_ref[...])
        l_ref[...] = jnp.where(here, jnp.sum(p, axis=-1, keepdims=True), l_ref[...])
        acc_ref[n] = _nn(_b(p), vv)

    @pl.when(j == nblk // bps - 1)
    def _():
        gate = gate_ref[...]
        rank = jnp.zeros((R, LANES), jnp.int32)
        for d in range(1, nblk):
            other = pltpu.roll(gate, LANES - d, 1)
            wraps = (lane & (nblk - 1)) + d >= nblk
            rank = rank + jnp.where((other > gate) | ((other == gate) & wraps), 1, 0)
        sel = (rank < min(MOBA_TOPK, nblk + 1)) & (lane < nblk)
        s_own = _nt(qs, _b(kn_ref[...]))
        qrow = _iota((R, SAMPLE_PAD), 0) & (SAMPLE_PAD - 1)
        s_own = jnp.where(_iota((R, SAMPLE_PAD), 1) <= qrow, s_own, NEG)
        m_own = jnp.max(s_own, axis=-1, keepdims=True)
        m_all = jnp.maximum(jnp.max(jnp.where(sel, m_ref[...], NEG), axis=-1, keepdims=True), m_own)
        p_own = jnp.exp(s_own - m_all)
        w = jnp.where(sel, jnp.exp(m_ref[...] - m_all), 0.0)
        denom = jnp.sum(w * l_ref[...], axis=-1, keepdims=True) + jnp.sum(p_own, axis=-1, keepdims=True)
        o = _nn(_b(p_own), _b(vn_ref[...]))
        for n in range(nblk):
            o = o + w[:, n:n + 1] * acc_ref[n]
        o_ref[...] = _unstack_heads(o * (1.0 / denom)) * _silu(g_ref[...])


def _moba_sample(page_table, k_pool, v_pool, layer, q, kn, vn, g, *, B, name):
    n_pages = page_table.shape[1]
    nblk = n_pages // 2
    assert nblk * 2 == n_pages and LANES % nblk == 0
    bps = min(2, nblk)
    R = N_HEADS * SAMPLE_PAD

    def page_spec(r):
        return pl.BlockSpec((None, None, LANES, 256), lambda b, j, pt: (layer, pt[b, j * 2 * bps + r], 0, 0))

    tok = pl.BlockSpec((SAMPLE_PAD, 256), lambda b, j, pt: (b, 0))
    grid_spec = pltpu.PrefetchScalarGridSpec(
        num_scalar_prefetch=1,
        grid=(B, nblk // bps),
        in_specs=[page_spec(r) for r in range(2 * bps)] * 2 + [tok, tok, tok, tok],
        out_specs=tok,
        scratch_shapes=[pltpu.VMEM((R, 256), BF16), pltpu.VMEM((R, LANES), F32), pltpu.VMEM((R, LANES), F32),
                        pltpu.VMEM((R, LANES), F32), pltpu.VMEM((nblk, R, 256), F32)],
    )
    return pl.pallas_call(
        functools.partial(_moba_sample_body, bps=bps, nblk=nblk),
        grid_spec=grid_spec,
        out_shape=jax.ShapeDtypeStruct((B * SAMPLE_PAD, 256), F32),
        compiler_params=_params(2),
        name=name,
    )(page_table, *([k_pool] * (2 * bps)), *([v_pool] * (2 * bps)), q, kn, vn, g)


def _dsa_sample_body(pt_ref, *refs, ppa, ppb, n_a, n_b, n_sel, past):
    ikp = refs[0:ppa]
    kp = refs[ppa:ppa + ppb]
    vp = refs[ppa + ppb:ppa + 2 * ppb]
    (q_ref, kn_ref, vn_ref, g_ref, qi_ref, wi_ref, kin_ref, o_ref,
     qs_ref, qis_ref, wis_ref, sc_ref, bias_ref, m_ref, l_ref, acc_ref) = refs[ppa + 2 * ppb:]
    j = pl.program_id(1)
    R = N_HEADS * SAMPLE_PAD
    RI = DSA_IDX_HEADS * SAMPLE_PAD
    nkb = sc_ref.shape[0]

    def index_scores(ik_bf16):
        a = jnp.maximum(_nt(qis_ref[...], ik_bf16), 0.0) * wis_ref[:, 0:1]
        out = a[0:SAMPLE_PAD, :]
        for h in range(1, DSA_IDX_HEADS):
            out = out + a[h * SAMPLE_PAD:(h + 1) * SAMPLE_PAD, :]
        return out

    @pl.when(j == 0)
    def _():
        qs_ref[...] = _b(_stack_heads(q_ref[...]))
        qi = qi_ref[...]
        wi = wi_ref[...]
        qis_ref[...] = _b(jnp.concatenate([qi[:, h * 64:(h + 1) * 64] for h in range(DSA_IDX_HEADS)], axis=0))
        wcol = jnp.concatenate([wi[:, h:h + 1] for h in range(DSA_IDX_HEADS)], axis=0)
        wis_ref[...] = jnp.broadcast_to(wcol, (RI, LANES))
        m_ref[...] = jnp.full(m_ref.shape, NEG, F32)
        l_ref[...] = jnp.zeros_like(l_ref)
        acc_ref[...] = jnp.zeros_like(acc_ref)

    @pl.when(j < n_a)
    def _():
        for r in range(ppa // 2):
            ik = _b(jnp.concatenate([ikp[2 * r][...], ikp[2 * r + 1][...]], axis=0))
            sc_ref[j * (ppa // 2) + r] = index_scores(ik)

    @pl.when(j == n_a - 1)
    def _():
        s_new = index_scores(_b(kin_ref[...]))
        ok = _iota((SAMPLE_PAD, SAMPLE_PAD), 1) <= _iota((SAMPLE_PAD, SAMPLE_PAD), 0)
        s_new = jnp.where(ok, s_new, -jnp.inf)
        pad = jnp.full((SAMPLE_PAD, MOBA_BLOCK - SAMPLE_PAD), -jnp.inf, F32)
        sc_ref[nkb - 1] = jnp.concatenate([s_new, pad], axis=1)
        keys = _sortable_keys(sc_ref[...])
        count = lambda m: jnp.sum(jnp.sum(m, axis=0, keepdims=True), axis=2, keepdims=True)
        thr = _kth_largest(keys, n_sel, count)
        neg_inf_key = jnp.int32(_sortable_neg_inf())
        cnt_ge = count(jnp.where(keys >= thr, 1, 0))
        cnt_gt = count(jnp.where(keys > thr, 1, 0))
        few = (past + _iota((1, SAMPLE_PAD, 1), 1) + 1) <= n_sel
        bias_ref[...] = jnp.where(few, jnp.where(keys > neg_inf_key, 0.0, NEG), jnp.where(keys >= thr, 0.0, NEG))
        tied = jnp.where(few, 0, jnp.where(cnt_ge > n_sel, 1, 0))

        @pl.when(jnp.max(tied) > 0)
        def _():
            need = (n_sel - cnt_gt[0]).astype(F32)
            thr2 = thr[0]
            few2 = few[0]
            tri = jnp.where(_iota((LANES, LANES), 0) <= _iota((LANES, LANES), 1), 1.0, 0.0).astype(BF16)

            def blk_body(n, seen):
                kb = _sortable_keys(sc_ref[n])
                for c in range(MOBA_BLOCK // LANES):
                    cs = slice(c * LANES, (c + 1) * LANES)
                    kc = kb[:, cs]
                    eq = jnp.where(kc == thr2, 1.0, 0.0)
                    upto = _nn(_b(eq), tri) + seen
                    pick = jnp.where(kc > thr2, 1.0, jnp.where(upto <= need, eq, 0.0))
                    tie_bias = jnp.where(pick > 0.5, 0.0, NEG)
                    bias_ref[n, :, cs] = jnp.where(few2, jnp.where(kc > neg_inf_key, 0.0, NEG), tie_bias)
                    seen = seen + jnp.sum(eq, axis=1, keepdims=True)
                return seen

            lax.fori_loop(0, nkb, blk_body, jnp.zeros((SAMPLE_PAD, 1), F32))

    def attend(kk, vv, bias):
        s = _nt(qs_ref[...], kk) + jnp.concatenate([bias] * N_HEADS, axis=0)
        m_old = m_ref[:, 0:1]
        m_new = jnp.maximum(m_old, jnp.max(s, axis=-1, keepdims=True))
        alpha = jnp.exp(m_old - m_new)
        p = jnp.exp(s - m_new)
        l_ref[...] = jnp.broadcast_to(alpha * l_ref[:, 0:1] + jnp.sum(p, axis=-1, keepdims=True), l_ref.shape)
        acc_ref[...] = alpha * acc_ref[...] + _nn(_b(p), vv)
        m_ref[...] = jnp.broadcast_to(m_new, m_ref.shape)

    @pl.when(j >= n_a)
    def _():
        for r in range(ppb // 2):
            kk = _b(jnp.concatenate([kp[2 * r][...], kp[2 * r + 1][...]], axis=0))
            vv = _b(jnp.concatenate([vp[2 * r][...], vp[2 * r + 1][...]], axis=0))
            attend(kk, vv, bias_ref[(j - n_a) * (ppb // 2) + r])

    @pl.when(j == n_a + n_b - 1)
    def _():
        attend(_b(kn_ref[...]), _b(vn_ref[...]), bias_ref[nkb - 1][:, 0:SAMPLE_PAD])
        o = acc_ref[...] * (1.0 / l_ref[:, 0:1])
        o_ref[...] = _unstack_heads(o) * _silu(g_ref[...])


def _sortable_neg_inf():
    return int(np.array(-np.inf, np.float32).view(np.int32)) ^ 0x7FFFFFFF


def _dsa_sample(page_table, k_pool, v_pool, ik_pool, layer, q, kn, vn, g, qi, wi, kin, *, B, valid, name):
    n_pages = page_table.shape[1]
    past = n_pages * LANES
    ppa = min(8, n_pages)
    ppb = min(4, n_pages)
    n_a = n_pages // ppa
    n_b = n_pages // ppb
    assert n_a * ppa == n_pages and n_b * ppb == n_pages and ppa % 2 == 0 and ppb % 2 == 0
    n_sel = max(1, min(DSA_TOPK, (past + valid) // 4))
    nkb = past // MOBA_BLOCK + 1
    R = N_HEADS * SAMPLE_PAD
    RI = DSA_IDX_HEADS * SAMPLE_PAD

    def ik_spec(r):
        return pl.BlockSpec((None, None, LANES, 64),
                            lambda b, j, pt: (layer, pt[b, jnp.minimum(j, n_a - 1) * ppa + r], 0, 0))

    def kv_spec(r):
        return pl.BlockSpec((None, None, LANES, 256),
                            lambda b, j, pt: (layer, pt[b, jnp.maximum(j - n_a, 0) * ppb + r], 0, 0))

    tok = lambda w: pl.BlockSpec((SAMPLE_PAD, w), lambda b, j, pt: (b, 0))
    grid_spec = pltpu.PrefetchScalarGridSpec(
        num_scalar_prefetch=1,
        grid=(B, n_a + n_b),
        in_specs=[ik_spec(r) for r in range(ppa)] + [kv_spec(r) for r in range(ppb)] * 2
        + [tok(256), tok(256), tok(256), tok(256), tok(512), tok(128), tok(64)],
        out_specs=tok(256),
        scratch_shapes=[pltpu.VMEM((R, 256), BF16), pltpu.VMEM((RI, 64), BF16), pltpu.VMEM((RI, LANES), F32),
                        pltpu.VMEM((nkb, SAMPLE_PAD, MOBA_BLOCK), F32), pltpu.VMEM((nkb, SAMPLE_PAD, MOBA_BLOCK), F32),
                        pltpu.VMEM((R, LANES), F32), pltpu.VMEM((R, LANES), F32), pltpu.VMEM((R, 256), F32)],
    )
    return pl.pallas_call(
        functools.partial(_dsa_sample_body, ppa=ppa, ppb=ppb, n_a=n_a, n_b=n_b, n_sel=n_sel, past=past),
        grid_spec=grid_spec,
        out_shape=jax.ShapeDtypeStruct((B * SAMPLE_PAD, 256), F32),
        compiler_params=_params(2),
        name=name,
    )(page_table, *([ik_pool] * ppa), *([k_pool] * ppb), *([v_pool] * ppb), q, kn, vn, g, qi, wi, kin)


def _out_proj_body(a_ref, b_ref, c_ref, d_ref, x_ref, wo_ref, g1_ref, g2_ref, wq_ref, x1_ref, q_ref):
    acc = _nn(_b(a_ref[...]), wo_ref[0:256, :])
    acc = acc + _nn(_b(b_ref[...]), wo_ref[256:512, :])
    acc = acc + _nn(_b(c_ref[...]), wo_ref[512:768, :])
    acc = acc + _nn(_b(d_ref[...]), wo_ref[768:1024, :])
    x1 = x_ref[...] + _rms(acc, g1_ref[...])
    x1_ref[...] = x1
    q_ref[...] = _nn(_b(_rms(x1, g2_ref[...])), wq_ref[...]) * (XA_DIM ** -0.5)


def _out_proj(parts, x, wo, g1, g2, wq, *, name):
    n, d = x.shape
    tm = min(256, n)
    tok = lambda w: pl.BlockSpec((tm, w), lambda i: (i, 0))
    const = lambda a: pl.BlockSpec(a.shape, lambda i: (0,) * a.ndim)
    return pl.pallas_call(
        _out_proj_body,
        grid=(n // tm,),
        in_specs=[tok(256)] * 4 + [tok(d), const(wo), const(g1), const(g2), const(wq)],
        out_specs=[tok(d), tok(d)],
        out_shape=[jax.ShapeDtypeStruct((n, d), F32)] * 2,
        compiler_params=_params(1),
        name=name,
    )(*parts, x, wo, g1, g2, wq)


def _xattn_body(q_ref, mk_ref, mv_ref, wo_ref, g_ref, x_ref, o_ref):
    acc = jnp.zeros(x_ref.shape, F32)
    for h in range(N_HEADS):
        hs = slice(h * XA_DIM, (h + 1) * XA_DIM)
        s = _nt(_b(q_ref[:, hs]), _b(mk_ref[:, hs]))
        m = jnp.max(s, axis=-1, keepdims=True)
        p = jnp.exp(s - m)
        oh = _nn(_b(p), _b(mv_ref[:, hs])) * (1.0 / jnp.sum(p, axis=-1, keepdims=True))
        acc = acc + _nn(_b(oh), wo_ref[hs, :])
    o_ref[...] = x_ref[...] + _rms(acc, g_ref[...])


def _xattn(q, mem_k, mem_v, layer, wo, g, x, *, B, L, name):
    T = min(256, L)
    nt = L // T
    d = x.shape[1]
    m = mem_k.shape[2]
    tok = pl.BlockSpec((T, d), lambda b, t: (b * nt + t, 0))
    mem = pl.BlockSpec((None, None, m, d), lambda b, t: (layer, b, 0, 0))
    const = lambda a: pl.BlockSpec(a.shape, lambda b, t: (0,) * a.ndim)
    return pl.pallas_call(
        _xattn_body,
        grid=(B, nt),
        in_specs=[tok, mem, mem, const(wo), const(g), tok],
        out_specs=tok,
        out_shape=jax.ShapeDtypeStruct(x.shape, F32),
        compiler_params=_params(2),
        name=name,
    )(q, mem_k, mem_v, wo, g, x)


def _pad_cols(w, width):
    return jnp.pad(w, ((0, 0), (0, width - w.shape[1])))


_IN_SEGS = (
    ("g_q", 0, 128, False, GLA_DK ** -0.5, 128), ("g_k", 1, 128, False, 1.0, 128), ("g_v", 2, 256, False, 1.0, 256),
    ("g_gate", 3, 256, False, 1.0, 256), ("g_lr", 4, 128, False, 1.0, 128),
    ("s_z", 5, 256, False, 1.0, 256), ("s_xbc", 6, 768, False, 1.0, 768), ("s_dt", 7, 128, False, 1.0, 128),
    ("m_q", 8, 256, True, HEAD_DIM ** -0.5, 256), ("m_k", 9, 256, True, 1.0, 256), ("m_v", 10, 256, False, 1.0, 256),
    ("m_gate", 11, 256, False, 1.0, 256),
    ("d_q", 12, 256, True, HEAD_DIM ** -0.5, 256), ("d_k", 13, 256, True, 1.0, 256), ("d_v", 14, 256, False, 1.0, 256),
    ("d_gate", 15, 256, False, 1.0, 256), ("d_qi", 16, 512, True, HEAD_DIM ** -0.5, 512),
    ("d_ki", 17, 128, True, 1.0, 64), ("d_wi", 18, 128, False, DSA_IDX_HEADS ** -0.5, 128),
)


def _in_proj_layout(w_in_l):
    offs = np.concatenate([[0], np.cumsum(IN_SPLITS)])
    cols, segs, c0 = [], [], 0
    for _, idx, width, rope, scale, ow in _IN_SEGS:
        cols.append(_pad_cols(w_in_l[:, offs[idx]:offs[idx + 1]], width))
        segs.append((c0, width, rope, scale, ow))
        c0 += width
    return _b(jnp.concatenate(cols, axis=1)), tuple(segs)


def _rope_tables(pos, rows):
    half = HEAD_DIM // 2
    inv = ROPE_THETA ** (-jnp.arange(half, dtype=F32) / half)
    ang = pos.astype(F32)[:, None] * inv[None, :]
    cos, sin = jnp.cos(ang), jnp.sin(ang)
    reps = 512 // HEAD_DIM
    cs = jnp.tile(jnp.concatenate([cos, cos], axis=1), (1, reps))
    sn = jnp.tile(jnp.concatenate([-sin, sin], axis=1), (1, reps))
    if cs.shape[0] < rows:
        cs = jnp.tile(cs, (rows // cs.shape[0], 1))
        sn = jnp.tile(sn, (rows // sn.shape[0], 1))
    return cs, sn


def _head_rep(v4, widths):
    return jnp.concatenate([jnp.repeat(v4, w) for w in widths]).reshape(1, -1)


def _layer_consts(l, W):
    w_cat, segs = _in_proj_layout(W["w_in"][l])
    e = np.zeros((128, 768), np.float32)
    for h in range(N_HEADS):
        e[h, h * 64:(h + 1) * 64] = 1.0
        e[h, 256 + h * 128:256 + (h + 1) * 128] = 1.0
    ssd_consts = (jnp.asarray(e),
                  jnp.pad(W["ssd_conv_w"][l], ((0, 8 - SSD_CONV), (0, 0))),
                  W["ssd_conv_b"][l].reshape(1, -1),
                  _head_rep(W["ssd_dt_bias"][l], (64, 128)),
                  _head_rep(W["ssd_a_log"][l], (64, 128)),
                  _head_rep(W["ssd_d"][l], (64,)),
                  W["ssd_norm"][l].reshape(1, -1))
    return dict(
        w_cat=w_cat, segs=segs,
        wlr=jnp.pad(W["gla_w_lr"][l], ((0, 128 - GLA_RANK), (0, 0))),
        blr=W["gla_b_lr"][l].reshape(1, -1),
        gn=jnp.tile(W["gla_norm"][l], N_HEADS).reshape(1, -1),
        ssd=ssd_consts,
        wo=_b(W["w_out"][l]), wq=_b(W["xa_wq"][l]), xwo=_b(W["xa_wo"][l]),
        wkv=_b(jnp.concatenate([W["xa_wk"][l], W["xa_wv"][l]], axis=1)),
    )


def _gla_state_to_t(s):
    B = s.shape[0]
    out = jnp.zeros((B, N_HEADS, 64, N_HEADS, GLA_DK), F32)
    for h in range(N_HEADS):
        out = out.at[:, h, :, h, :].set(jnp.swapaxes(s[:, h], 1, 2))
    return out.reshape(B, 256, 128)


def _gla_state_from_t(st):
    B = st.shape[0]
    s5 = st.reshape(B, N_HEADS, 64, N_HEADS, GLA_DK)
    return jnp.stack([jnp.swapaxes(s5[:, h, :, h, :], 1, 2) for h in range(N_HEADS)], axis=1)


def _mixer_layer(x, l, C, W, tabs, *, B, L, valid, gla_s0t, ssd_s0, ssd_s0_idx, conv0, mem_k, mem_v, mem_idx,
                 sample_ctx, tag):
    outs = _norm_proj(x, W["norm_mix_pre"][l], C["w_cat"], C["segs"], tabs, name=f"in_proj_{tag}")
    P = {seg[0]: o for seg, o in zip(_IN_SEGS, outs)}
    o_gla, gla_st = _gla(P["g_q"], P["g_k"], P["g_v"], P["g_gate"], P["g_lr"], C["wlr"], C["blr"], C["gn"], gla_s0t,
                         B=B, L=L, valid=valid, name=f"gla_{tag}")
    o_ssd, ssd_s, conv_new = _ssd(P["s_z"], P["s_xbc"], P["s_dt"], conv0, C["ssd"], ssd_s0, ssd_s0_idx,
                                  B=B, L=L, valid=valid, name=f"ssd_{tag}")
    if sample_ctx is None:
        o_moba = _moba_prompt(P["m_q"], P["m_k"], P["m_v"], P["m_gate"], B=B, L=L, name=f"moba_{tag}")
        o_dsa = _dsa_prompt(P["d_q"], P["d_k"], P["d_v"], P["d_gate"], P["d_qi"], P["d_wi"], P["d_ki"],
                            B=B, L=L, name=f"dsa_{tag}")
    else:
        pt, mk_pool, mv_pool, dk_pool, dv_pool, ik_pool = sample_ctx
        o_moba = _moba_sample(pt, mk_pool, mv_pool, l, P["m_q"], P["m_k"], P["m_v"], P["m_gate"], B=B,
                              name=f"moba_{tag}")
        o_dsa = _dsa_sample(pt, dk_pool, dv_pool, ik_pool, l, P["d_q"], P["d_k"], P["d_v"], P["d_gate"],
                            P["d_qi"], P["d_wi"], P["d_ki"], B=B, valid=valid, name=f"dsa_{tag}")
    x1, xq = _out_proj((o_gla, o_ssd, o_moba, o_dsa), x, C["wo"], W["norm_mix_post"][l].reshape(1, -1),
                       W["norm_xa_pre"][l].reshape(1, -1), C["wq"], name=f"out_proj_{tag}")
    x2 = _xattn(xq, mem_k, mem_v, mem_idx, C["xwo"], W["norm_xa_post"][l].reshape(1, -1), x1, B=B, L=L,
                name=f"xattn_{tag}")
    new = (gla_st, ssd_s, conv_new, P["m_k"], P["m_v"], P["d_k"], P["d_v"], P["d_ki"])
    return x2, new


def kernel(x_prompt, x_sample, state_gla, state_ssd, state_conv, cache_moba_k, cache_moba_v,
           cache_dsa_k, cache_dsa_v, cache_dsa_ik, cache_mem_k, cache_mem_v, page_table, mem_prompt,
           norm_mix_pre, norm_mix_post, norm_xa_pre, norm_xa_post, norm_mem, w_in, gla_w_lr, gla_b_lr,
           gla_norm, ssd_conv_w, ssd_conv_b, ssd_dt_bias, ssd_a_log, ssd_d, ssd_norm, w_out,
           xa_wq, xa_wk, xa_wv, xa_wo):
    W = dict(norm_mix_pre=norm_mix_pre, norm_mix_post=norm_mix_post, norm_xa_pre=norm_xa_pre,
             norm_xa_post=norm_xa_post, norm_mem=norm_mem, w_in=w_in, gla_w_lr=gla_w_lr, gla_b_lr=gla_b_lr,
             gla_norm=gla_norm, ssd_conv_w=ssd_conv_w, ssd_conv_b=ssd_conv_b, ssd_dt_bias=ssd_dt_bias,
             ssd_a_log=ssd_a_log, ssd_d=ssd_d, ssd_norm=ssd_norm, w_out=w_out, xa_wq=xa_wq, xa_wk=xa_wk,
             xa_wv=xa_wv, xa_wo=xa_wo)
    depth = w_in.shape[0]
    Bp, S, D = x_prompt.shape
    Bs, Ls, _ = x_sample.shape
    M = mem_prompt.shape[1]
    n_pool, page = cache_moba_k.shape[1], cache_moba_k.shape[2]
    assert page == LANES and Ls <= SAMPLE_PAD
    past = page_table.shape[1] * page
    consts = [_layer_consts(l, W) for l in range(depth)]

    tabs_p = _rope_tables(jnp.arange(S, dtype=jnp.int32), min(256, Bp * S))
    zero_gla = jnp.zeros((Bp, 256, 128), F32)
    zero_ssd = jnp.zeros((1, Bp, 256, 128), F32)
    zero_conv = jnp.zeros((Bp, 8, 768), F32)
    mem_flat = mem_prompt.reshape(Bp * M, D)
    kv_segs = ((0, D, False, 1.0, D), (D, D, False, 1.0, D))
    x = x_prompt.reshape(Bp * S, D)
    p_new = []
    for l in range(depth):
        C = consts[l]
        mk, mv = _norm_proj(mem_flat, norm_mem[l], C["wkv"], kv_segs, name="mem_kv")
        x, new = _mixer_layer(x, l, C, W, tabs_p, B=Bp, L=S, valid=S, gla_s0t=zero_gla, ssd_s0=zero_ssd,
                              ssd_s0_idx=0, conv0=zero_conv, mem_k=mk.reshape(1, Bp, M, D),
                              mem_v=mv.reshape(1, Bp, M, D), mem_idx=0, sample_ctx=None, tag="p")
        p_new.append(new + (mk, mv))
    y_prompt = x.reshape(Bp, S, D)
    (p_gla, p_ssd, p_conv, p_mk, p_mv, p_dk, p_dv, p_ik, p_memk, p_memv) = [jnp.stack(t) for t in zip(*p_new)]
    p_gla = _gla_state_from_t(p_gla.reshape(depth * Bp, 256, 128)).reshape(depth, Bp, N_HEADS, GLA_DK, 64)
    p_ssd = p_ssd.reshape(depth, Bp, N_HEADS, 64, SSD_N)
    p_conv = p_conv[:, :, :SSD_CONV - 1]
    kv5 = lambda a: a.reshape(depth, Bp, S, N_HEADS, HEAD_DIM)
    mem5 = lambda a: a.reshape(depth, Bp, M, N_HEADS, XA_DIM)

    Lp = SAMPLE_PAD
    tabs_s = _rope_tables(past + jnp.arange(Lp, dtype=jnp.int32), min(256, Bs * Lp))
    xs = jnp.pad(x_sample, ((0, 0), (0, Lp - Ls), (0, 0))).reshape(Bs * Lp, D)
    gla_s0_all = _gla_state_to_t(state_gla.reshape(depth * Bs, N_HEADS, GLA_DK, 64)).reshape(depth, Bs, 256, 128)
    ssd_s0_all = state_ssd.reshape(depth, Bs, 256, SSD_N)
    conv0_all = jnp.pad(state_conv, ((0, 0), (0, 0), (8 - (SSD_CONV - 1), 0), (0, 0)))
    pools = tuple(a.reshape(depth, n_pool, page, -1) for a in
                  (cache_moba_k, cache_moba_v, cache_dsa_k, cache_dsa_v, cache_dsa_ik))
    memk_all = cache_mem_k.reshape(depth, Bs, M, D)
    memv_all = cache_mem_v.reshape(depth, Bs, M, D)
    s_new = []
    for l in range(depth):
        xs, new = _mixer_layer(xs, l, consts[l], W, tabs_s, B=Bs, L=Lp, valid=Ls, gla_s0t=gla_s0_all[l],
                               ssd_s0=ssd_s0_all, ssd_s0_idx=l, conv0=conv0_all[l], mem_k=memk_all, mem_v=memv_all,
                               mem_idx=l, sample_ctx=(page_table,) + pools, tag="s")
        s_new.append(new)
    y_sample = xs.reshape(Bs, Lp, D)[:, :Ls]
    (s_gla, s_ssd, s_conv, s_mk, s_mv, s_dk, s_dv, s_ik) = [jnp.stack(t) for t in zip(*s_new)]
    s_gla = _gla_state_from_t(s_gla.reshape(depth * Bs, 256, 128)).reshape(depth, Bs, N_HEADS, GLA_DK, 64)
    s_ssd = s_ssd.reshape(depth, Bs, N_HEADS, 64, SSD_N)
    s_conv = s_conv[:, :, :SSD_CONV - 1]
    skv5 = lambda a: a.reshape(depth, Bs, Lp, N_HEADS, HEAD_DIM)[:, :, :Ls]

    return (y_prompt, y_sample, p_gla, p_ssd, p_conv, kv5(p_mk), kv5(p_mv), kv5(p_dk), kv5(p_dv),
            p_ik.reshape(depth, Bp, S, HEAD_DIM), mem5(p_memk), mem5(p_memv),
            s_gla, s_ssd, s_conv, skv5(s_mk), skv5(s_mv), skv5(s_dk), skv5(s_dv),
            s_ik.reshape(depth, Bs, Lp, HEAD_DIM)[:, :, :Ls])
```

```python
import functools

import jax
import jax.numpy as jnp
import numpy as np
from jax import lax
from jax.experimental import pallas as pl
from jax.experimental.pallas import tpu as pltpu

F32 = jnp.float32
BF16 = jnp.bfloat16
HI = lax.Precision.HIGHEST

D_MODEL = 1024
HEAD_DIM = 64
ROPE_THETA = 10000.0
EPS = 1e-6
N_HEADS = 4
GLA_DK = 32
GLA_RANK = 16
GLA_TAU = 16.0
SCAN_CHUNK = 64
SSD_N = 128
SSD_CONV = 4
MOBA_BLOCK = 256
MOBA_TOPK = 3
DSA_IDX_HEADS = 8
DSA_TOPK = 256
XA_DIM = 256
IN_SPLITS = (128, 128, 256, 256, 16, 256, 768, 4, 256, 256, 256, 256, 256, 256, 256, 256, 512, 64, 8)

LANES = 128
SUBLANES = 8
SAMPLE_PAD = SUBLANES
SAMPLE_PAGES_PER_STEP = 16
VMEM_LIMIT = 56 * 1024 * 1024
NEG = -1e30
_INT_MIN = -2 ** 31


def _nn(a, b, precision=None):
    return jnp.dot(a, b, preferred_element_type=F32, precision=precision)


def _nt(a, b, precision=None):
    return lax.dot_general(a, b, (((1,), (1,)), ((), ())), preferred_element_type=F32, precision=precision)


def _tn(a, b, precision=None):
    return lax.dot_general(a, b, (((0,), (0,)), ((), ())), preferred_element_type=F32, precision=precision)


def _b(x):
    return x.astype(BF16)


def _sigmoid(x):
    return 1.0 / (1.0 + jnp.exp(-x))


def _silu(x):
    return x * _sigmoid(x)


def _softplus(x):
    return jnp.maximum(x, 0.0) + jnp.log1p(jnp.exp(-jnp.abs(x)))


def _log_sigmoid(x):
    return jnp.minimum(x, 0.0) - jnp.log1p(jnp.exp(-jnp.abs(x)))


def _rms(x, g):
    return x * lax.rsqrt(jnp.mean(x * x, axis=-1, keepdims=True) + EPS) * g


def _iota(shape, dim):
    return lax.broadcasted_iota(jnp.int32, shape, dim)


def _fold(x):
    return x[:, 0:LANES], x[:, LANES:2 * LANES]


def _params(n_axes):
    return pltpu.CompilerParams(dimension_semantics=("arbitrary",) * n_axes, vmem_limit_bytes=VMEM_LIMIT)


def _norm_proj_body(*refs, segs, tsegs, has_rope):
    refs = list(refs)
    x_ref, g_ref, w_ref = refs[:3]
    del refs[:3]
    wt_ref = refs.pop(0) if tsegs else None
    cs_ref, sn_ref = (refs.pop(0), refs.pop(0)) if has_rope else (None, None)
    cst_ref, snt_ref = (refs.pop(0), refs.pop(0)) if (has_rope and tsegs) else (None, None)
    outs = refs
    hb = _b(_rms(x_ref[...], g_ref[...]))
    for (c0, w, rope, scale, ow), o_ref in zip(segs, outs):
        y = _nn(hb, w_ref[:, c0:c0 + w])
        if rope:
            lane = _iota(y.shape, 1)
            sw = jnp.where((lane & 63) < 32, pltpu.roll(y, w - 32, 1), pltpu.roll(y, 32, 1))
            y = y * cs_ref[:, :w] + sw * sn_ref[:, :w]
        if scale != 1.0:
            y = y * scale
        o_ref[...] = y[:, :ow]
    for (r0, w, rope), o_ref in zip(tsegs, outs[len(segs):]):
        yt = _nt(wt_ref[r0:r0 + w, :], hb)
        if rope:
            halves = []
            for h in range(w // HEAD_DIM):
                halves += [yt[h * 64 + 32:h * 64 + 64, :], yt[h * 64:h * 64 + 32, :]]
            yt = yt * cst_ref[0:w, :] + jnp.concatenate(halves, axis=0) * snt_ref[0:w, :]
        o_ref[...] = yt


def _norm_proj(x, gain, w_cat, segs, rope_tabs=None, wt_cat=None, tsegs=(), rope_tabs_t=None, seq_len=None, *, name):
    n, d = x.shape
    tm = min(256, n)
    has_rope = rope_tabs is not None
    in_specs = [pl.BlockSpec((tm, d), lambda i: (i, 0)),
                pl.BlockSpec((1, d), lambda i: (0, 0)),
                pl.BlockSpec(w_cat.shape, lambda i: (0, 0))]
    args = [x, gain.reshape(1, d), w_cat]
    if tsegs:
        in_specs.append(pl.BlockSpec(wt_cat.shape, lambda i: (0, 0)))
        args.append(wt_cat)
    if has_rope:
        cs, sn = rope_tabs
        nt = cs.shape[0] // tm
        in_specs += [pl.BlockSpec((tm, cs.shape[1]), lambda i: (i % nt, 0))] * 2
        args += [cs, sn]
    out_specs = [pl.BlockSpec((tm, s[4]), lambda i: (i, 0)) for s in segs]
    out_shape = [jax.ShapeDtypeStruct((n, s[4]), F32) for s in segs]
    if tsegs:
        nts = seq_len // tm
        assert nts * tm == seq_len
        if has_rope:
            cst, snt = rope_tabs_t
            in_specs += [pl.BlockSpec((cst.shape[0], tm), lambda i: (0, i % nts))] * 2
            args += [cst, snt]
        out_specs += [pl.BlockSpec((None, s[1], tm), lambda i: (i // nts, 0, i % nts)) for s in tsegs]
        out_shape += [jax.ShapeDtypeStruct((n // seq_len, s[1], seq_len), F32) for s in tsegs]
    return pl.pallas_call(
        functools.partial(_norm_proj_body, segs=segs, tsegs=tuple(tsegs), has_rope=has_rope),
        grid=(n // tm,),
        in_specs=in_specs,
        out_specs=out_specs,
        out_shape=out_shape,
        compiler_params=_params(1),
        name=name,
    )(*args)


def _gla_body(q_ref, k_ref, v_ref, g_ref, lr_ref, wlr_ref, blr_ref, gn_ref, s0_ref, o_ref, so_ref, st_ref,
              *, T, C, valid, nt):
    t = pl.program_id(1)

    @pl.when(t == 0)
    def _():
        st_ref[...] = s0_ref[...]

    row = _iota((C, C), 0)
    col = _iota((C, C), 1)
    causal = col <= row
    tril = causal.astype(F32)
    lane_qk = _iota((1, 128), 1)
    lane_v = _iota((1, 256), 1)
    bd_state = (_iota((256, 128), 0) >> 6) == (_iota((256, 128), 1) >> 5)
    bd_head = ((_iota((256, 256), 0) >> 6) == (_iota((256, 256), 1) >> 6)).astype(F32)
    st = st_ref[...]
    for c in range(T // C):
        sl = slice(c * C, (c + 1) * C)
        q = q_ref[sl, :]
        k = k_ref[sl, :]
        v = v_ref[sl, :]
        logit = _nn(lr_ref[sl, :], wlr_ref[...], HI) + blr_ref[...]
        logf = _log_sigmoid(logit) * (1.0 / GLA_TAU)
        if valid < T:
            live = _iota((C, 1), 0) < valid
            logf = jnp.where(live, logf, 0.0)
            k = jnp.where(live, k, 0.0)
        gcum = _nn(tril, logf, HI)
        gend = gcum[C - 1:C, :]
        qd = q * jnp.exp(gcum)
        kinv = k * jnp.exp(-gcum)
        kend = k * jnp.exp(gend - gcum)
        o = _nt(_b(qd), _b(st))
        for h in range(N_HEADS):
            att = _nt(_b(jnp.where((lane_qk >> 5) == h, qd, 0.0)), _b(kinv))
            att = jnp.where(causal, att, 0.0)
            o = o + _nn(_b(att), _b(jnp.where((lane_v >> 6) == h, v, 0.0)))
        st = st * jnp.exp(gend) + jnp.where(bd_state, _tn(_b(v), _b(kend)), 0.0)
        ms = _nn(o * o, bd_head, HI) * (1.0 / HEAD_DIM)
        o_ref[sl, :] = o * lax.rsqrt(ms + EPS) * gn_ref[...] * _silu(g_ref[sl, :])
    st_ref[...] = st

    @pl.when(t == nt - 1)
    def _():
        so_ref[...] = st


def _gla(q, k, v, g, lr, wlr, blr, gn, s0t, *, B, L, valid, name):
    T = min(256, L)
    C = min(SCAN_CHUNK, T)
    nt = L // T
    assert valid == L or nt == 1
    tok = lambda w: pl.BlockSpec((T, w), lambda b, t: (b * nt + t, 0))
    const = lambda a: pl.BlockSpec(a.shape, lambda b, t: (0,) * a.ndim)
    st_spec = pl.BlockSpec((None, 256, 128), lambda b, t: (b, 0, 0))
    return pl.pallas_call(
        functools.partial(_gla_body, T=T, C=C, valid=valid, nt=nt),
        grid=(B, nt),
        in_specs=[tok(128), tok(128), tok(256), tok(256), tok(128), const(wlr), const(blr), const(gn), st_spec],
        out_specs=[tok(256), st_spec],
        out_shape=[jax.ShapeDtypeStruct((B * L, 256), F32), jax.ShapeDtypeStruct((B, 256, 128), F32)],
        scratch_shapes=[pltpu.VMEM((256, 128), F32)],
        compiler_params=_params(2),
        name=name,
    )(q, k, v, g, lr, wlr, blr, gn, s0t)


def _ssd_body(z_ref, xbc_ref, dtr_ref, c0_ref, e_ref, cw_ref, cb_ref, dtb_ref, alog_ref, d_ref, nw_ref, s0_ref,
              o_ref, so_ref, cn_ref, s_ref, xb_ref, *, T, C, valid, nt):
    t = pl.program_id(1)

    @pl.when(t == 0)
    def _():
        s_ref[...] = s0_ref[...]
        xb_ref[0:8, :] = c0_ref[...]

    xb_ref[8:8 + T, :] = xbc_ref[...]
    conv = cb_ref[...] + xb_ref[5:5 + T, :] * cw_ref[0:1, :]
    for j in range(1, SSD_CONV):
        conv = conv + xb_ref[5 + j:5 + j + T, :] * cw_ref[j:j + 1, :]
    act = _silu(conv)
    dt = _softplus(_nn(dtr_ref[...], e_ref[...], HI) + dtb_ref[...])
    v_last = valid - (nt - 1) * T
    if v_last < T:
        dt = jnp.where(_iota((T, 1), 0) < v_last, dt, 0.0)
    lac = dt * (-jnp.exp(alog_ref[...]))

    row = _iota((C, C), 0)
    col = _iota((C, C), 1)
    causal = col <= row
    tril = causal.astype(F32)
    triu = (row <= col).astype(F32)
    lane_x = _iota((1, 256), 1)
    s_all = s_ref[...]
    for c in range(T // C):
        sl = slice(c * C, (c + 1) * C)
        x = act[sl, 0:256]
        bm = act[sl, 256:512]
        cm = act[sl, 512:768]
        xdt = x * dt[sl, 0:256]
        cum64 = _nn(tril, lac[sl, 0:256], HI)
        cum128 = _nn(tril, lac[sl, 256:768], HI)
        cum_t = _tn(lac[sl, 256:768], triu, HI)
        cb = [_nt(_b(cm[:, g * 128:(g + 1) * 128]), _b(bm[:, g * 128:(g + 1) * 128])) for g in range(2)]
        y = x * d_ref[...]
        for h in range(N_HEADS):
            seg = cum128[:, h * 128:h * 128 + C] - cum_t[h * 128:h * 128 + C, :]
            w = jnp.where(causal, cb[h // 2] * jnp.exp(jnp.where(causal, seg, 0.0)), 0.0)
            y = y + _nn(_b(w), _b(jnp.where((lane_x >> 6) == h, xdt, 0.0)))
        ys0 = _nt(_b(cm[:, 0:128]), _b(s_all))
        ys1 = _nt(_b(cm[:, 128:256]), _b(s_all))
        y = y + jnp.where(lane_x < 128, ys0, ys1) * jnp.exp(cum64)
        new_rows = []
        for h in range(N_HEADS):
            g = h // 2
            c_end = cum128[C - 1:C, h * 128:(h + 1) * 128]
            bw = bm[:, g * 128:(g + 1) * 128] * jnp.exp(c_end - cum128[:, h * 128:(h + 1) * 128])
            upd = _tn(_b(xdt[:, h * 64:(h + 1) * 64]), _b(bw))
            new_rows.append(s_all[h * 64:(h + 1) * 64, :] * jnp.exp(c_end) + upd)
        s_all = jnp.concatenate(new_rows, axis=0)
        o_ref[sl, :] = _rms(y * _silu(z_ref[sl, :]), nw_ref[...])
    s_ref[...] = s_all
    xb_ref[0:8, :] = xb_ref[T:T + 8, :]

    @pl.when(t == nt - 1)
    def _():
        so_ref[...] = s_all
        cn_ref[...] = jnp.zeros_like(cn_ref)
        cn_ref[0:SSD_CONV - 1, :] = xb_ref[8 + v_last - (SSD_CONV - 1):8 + v_last, :]


def _ssd(z, xbc, dtr, conv0, consts, s0, s0_index, *, B, L, valid, name):
    T = min(256, L)
    C = min(SCAN_CHUNK, T)
    nt = L // T
    assert valid == L or nt == 1
    tok = lambda w: pl.BlockSpec((T, w), lambda b, t: (b * nt + t, 0))
    const = lambda a: pl.BlockSpec(a.shape, lambda b, t: (0,) * a.ndim)
    l_idx = s0_index
    s0_spec = pl.BlockSpec((None, None, 256, 128), lambda b, t: (l_idx, b, 0, 0))
    so_spec = pl.BlockSpec((None, 256, 128), lambda b, t: (b, 0, 0))
    cv_spec = pl.BlockSpec((None, 8, 768), lambda b, t: (b, 0, 0))
    return pl.pallas_call(
        functools.partial(_ssd_body, T=T, C=C, valid=valid, nt=nt),
        grid=(B, nt),
        in_specs=[tok(256), tok(768), tok(128), cv_spec] + [const(a) for a in consts] + [s0_spec],
        out_specs=[tok(256), so_spec, cv_spec],
        out_shape=[jax.ShapeDtypeStruct((B * L, 256), F32), jax.ShapeDtypeStruct((B, 256, 128), F32),
                   jax.ShapeDtypeStruct((B, 8, 768), F32)],
        scratch_shapes=[pltpu.VMEM((256, 128), F32), pltpu.VMEM((T + 8, 768), F32)],
        compiler_params=_params(2),
        name=name,
    )(z, xbc, dtr, conv0, *consts, s0)


def _head_cols(h):
    return (_iota((1, 256), 1) >> 6) == h


def _score_block(n, qh, kb_ref, bias_of_head, s_ref, m_ref):
    kb = kb_ref[n]
    for h in range(N_HEADS):
        s = _nn(qh[h], kb[h * 64:(h + 1) * 64, :]) + bias_of_head(h)
        s_ref[h, n] = s
        lo, hi = _fold(s)
        m_ref[h] = jnp.maximum(m_ref[h], jnp.maximum(lo, hi))


def _value_block(n, m_rows, vb_ref, s_ref, l_ref, acc_ref):
    vb = vb_ref[n]
    tot = None
    for h in range(N_HEADS):
        p = jnp.exp(s_ref[h, n] - m_rows[h])
        lo, hi = _fold(p)
        l_ref[h] = l_ref[h] + lo + hi
        pv = jnp.where(_head_cols(h), _nt(_b(p), vb), 0.0)
        tot = pv if tot is None else tot + pv
    acc_ref[...] = acc_ref[...] + tot


def _finish_heads(l_ref, acc_ref):
    inv = jnp.zeros(acc_ref.shape, F32)
    for h in range(N_HEADS):
        inv = inv + jnp.where(_head_cols(h), 1.0 / jnp.sum(l_ref[h], axis=-1, keepdims=True), 0.0)
    return acc_ref[...] * inv


def _moba_prompt_body(q_ref, kt_ref, vt_ref, g_ref, o_ref, kb_ref, vb_ref, km_ref, s_ref, m_ref, l_ref, acc_ref,
                      *, L, nb):
    i = pl.program_id(1)
    T = MOBA_BLOCK
    nbp = km_ref.shape[0]

    @pl.when(i == 0)
    def _():
        for n in range(nb):
            kb_ref[n] = _b(kt_ref[:, n * T:(n + 1) * T])
            vb_ref[n] = _b(vt_ref[:, n * T:(n + 1) * T])
        mean_of_block = jnp.where((_iota((nbp, L), 1) >> 8) == _iota((nbp, L), 0), 1.0 / T, 0.0)
        km_ref[...] = _nt(mean_of_block, kt_ref[...], HI)

    q = q_ref[...]
    km = km_ref[...]
    blk = _iota((nbp, T), 0)
    n_sel = min(MOBA_TOPK, nb)
    sel_cols = []
    for h in range(N_HEADS):
        gt = _nt(jnp.where(_head_cols(h), km, 0.0), q, HI)
        rank = jnp.zeros((nbp, T), jnp.int32)
        for n2 in range(nb):
            other = gt[n2:n2 + 1, :]
            beats = (other > gt) | ((other == gt) & (n2 < blk))
            rank = rank + jnp.where(beats, 1, 0) * jnp.where(n2 < i, 1, 0)
        sel_cols.append(jnp.where((blk < i) & (rank < n_sel), 1.0, 0.0).T)
    qh = [_b(q[:, h * 64:(h + 1) * 64]) for h in range(N_HEADS)]
    causal_bias = jnp.where(_iota((T, T), 1) <= _iota((T, T), 0), 0.0, NEG)

    m_ref[...] = jnp.full(m_ref.shape, NEG, F32)
    for n in range(nb - 1):
        @pl.when(n < i)
        def _():
            _score_block(n, qh, kb_ref, lambda h: jnp.where(sel_cols[h][:, n:n + 1] > 0.5, 0.0, NEG), s_ref, m_ref)
    _score_block(i, qh, kb_ref, lambda h: causal_bias, s_ref, m_ref)
    m_rows = [jnp.max(m_ref[h], axis=-1, keepdims=True) for h in range(N_HEADS)]

    l_ref[...] = jnp.zeros_like(l_ref)
    acc_ref[...] = jnp.zeros_like(acc_ref)
    for n in range(nb - 1):
        @pl.when(n < i)
        def _():
            _value_block(n, m_rows, vb_ref, s_ref, l_ref, acc_ref)
    _value_block(i, m_rows, vb_ref, s_ref, l_ref, acc_ref)
    o_ref[...] = _finish_heads(l_ref, acc_ref) * _silu(g_ref[...])


def _attn_scratch(T, nb):
    return [pltpu.VMEM((N_HEADS, nb, T, MOBA_BLOCK), F32), pltpu.VMEM((N_HEADS, T, LANES), F32),
            pltpu.VMEM((N_HEADS, T, LANES), F32), pltpu.VMEM((T, 256), F32)]


def _moba_prompt(q, kt, vt, g, *, B, L, name):
    T = MOBA_BLOCK
    nb = L // T
    nbp = -(-nb // 8) * 8
    tok = pl.BlockSpec((T, 256), lambda b, i: (b * nb + i, 0))
    seq = pl.BlockSpec((None, 256, L), lambda b, i: (b, 0, 0))
    return pl.pallas_call(
        functools.partial(_moba_prompt_body, L=L, nb=nb),
        grid=(B, nb),
        in_specs=[tok, seq, seq, tok],
        out_specs=tok,
        out_shape=jax.ShapeDtypeStruct((B * L, 256), F32),
        scratch_shapes=[pltpu.VMEM((nb, 256, T), BF16), pltpu.VMEM((nb, 256, T), BF16), pltpu.VMEM((nbp, 256), F32)]
        + _attn_scratch(T, nb),
        compiler_params=_params(2),
        name=name,
    )(q, kt, vt, g)


def _sortable_keys(score):
    bits = pltpu.bitcast(score, jnp.int32)
    keys = jnp.where(bits < 0, bits ^ jnp.int32(0x7FFFFFFF), bits)
    return jnp.where(score == 0.0, 0, keys)


def _sortable_neg_inf():
    return int(np.array(-np.inf, np.float32).view(np.int32)) ^ 0x7FFFFFFF


def _kth_largest(k, count_ge, zero):
    def body(i, ans):
        cand = ans | jnp.left_shift(jnp.int32(1), 31 - i)
        return jnp.where(count_ge(cand ^ jnp.int32(_INT_MIN)) >= k, cand, ans)

    return lax.fori_loop(0, 32, body, zero) ^ jnp.int32(_INT_MIN)


def _dsa_prompt_body(q_ref, kt_ref, vt_ref, g_ref, qi_ref, wi_ref, kit_ref, o_ref,
                     kb_ref, vb_ref, kib_ref, keys_ref, bias_ref, thr_ref, cge_ref, cgt_ref,
                     s_ref, m_ref, l_ref, acc_ref, *, L, nb, n_sel):
    i = pl.program_id(1)
    T = MOBA_BLOCK

    @pl.when(i == 0)
    def _():
        for n in range(nb):
            kb_ref[n] = _b(kt_ref[:, n * T:(n + 1) * T])
            vb_ref[n] = _b(vt_ref[:, n * T:(n + 1) * T])
            kib_ref[n] = _b(kit_ref[:, n * T:(n + 1) * T])

    qi = qi_ref[...]
    wi = wi_ref[...]
    qib = [_b(qi[:, h * 64:(h + 1) * 64]) for h in range(DSA_IDX_HEADS)]
    wcol = [wi[:, h:h + 1] for h in range(DSA_IDX_HEADS)]

    def index_block(n):
        kib = kib_ref[n]
        sc = wcol[0] * jnp.maximum(_nn(qib[0], kib), 0.0)
        for h in range(1, DSA_IDX_HEADS):
            sc = sc + wcol[h] * jnp.maximum(_nn(qib[h], kib), 0.0)
        return sc

    for n in range(nb - 1):
        @pl.when(n < i)
        def _():
            keys_ref[n] = _sortable_keys(index_block(n))
    in_past = _iota((T, T), 1) <= _iota((T, T), 0)
    keys_ref[i] = _sortable_keys(jnp.where(in_past, index_block(i), -jnp.inf))

    for n in range(1, nb):
        @pl.when(n > i)
        def _():
            keys_ref[n] = jnp.full((T, T), _INT_MIN, jnp.int32)

    ones = jnp.ones((T, LANES), BF16)
    wide = lambda x: jnp.concatenate([x, x], axis=1)

    def threshold_over(n_blocks):
        def count(pred):
            cnt = jnp.zeros((T, LANES), F32)
            for n in range(n_blocks):
                cnt = cnt + _nn(jnp.where(pred(keys_ref[n]), 1.0, 0.0).astype(BF16), ones)
            return cnt

        t = _kth_largest(n_sel, lambda c: count(lambda kk: kk >= wide(c)), jnp.zeros((T, LANES), jnp.int32))
        thr_ref[...] = t
        cge_ref[...] = count(lambda kk: kk >= wide(t))
        cgt_ref[...] = count(lambda kk: kk > wide(t))

    n_half = max(1, nb // 2)
    if n_half < nb:
        @pl.when(i < n_half)
        def _():
            threshold_over(n_half)

        @pl.when(i >= n_half)
        def _():
            threshold_over(nb)
    else:
        threshold_over(nb)
    thr = thr_ref[...]
    cnt_ge = cge_ref[...]
    cnt_gt = cgt_ref[...]
    few = (i * T + _iota((T, LANES), 0) + 1) <= n_sel
    neg_inf_key = jnp.int32(_sortable_neg_inf())
    thr_eff = wide(jnp.where(few, neg_inf_key + 1, thr))

    def write_bias(n, carry):
        bias_ref[n] = jnp.where(keys_ref[n] >= thr_eff, 0.0, NEG)
        return carry

    lax.fori_loop(0, i + 1, write_bias, 0)
    tied = jnp.where(few, 0.0, jnp.where(cnt_ge > n_sel, 1.0, 0.0))

    @pl.when(jnp.max(tied) > 0.5)
    def _():
        need = n_sel - cnt_gt
        tri = jnp.where(_iota((LANES, LANES), 0) <= _iota((LANES, LANES), 1), 1.0, 0.0).astype(BF16)
        ones_sq = jnp.ones((LANES, LANES), BF16)

        def tie_block(n, seen):
            for c in range(T // LANES):
                cs = slice(c * LANES, (c + 1) * LANES)
                kc = keys_ref[n, :, cs]
                eq = jnp.where(kc == thr, 1.0, 0.0)
                upto = _nn(_b(eq), tri) + seen
                pick = jnp.where(kc > thr, 1.0, jnp.where(upto <= need, eq, 0.0))
                tie_bias = jnp.where(pick > 0.5, 0.0, NEG)
                bias_ref[n, :, cs] = jnp.where(few, jnp.where(kc > neg_inf_key, 0.0, NEG), tie_bias)
                seen = seen + _nn(_b(eq), ones_sq)
            return seen

        lax.fori_loop(0, i + 1, tie_block, jnp.zeros((T, LANES), F32))

    q = q_ref[...]
    qh = [_b(q[:, h * 64:(h + 1) * 64]) for h in range(N_HEADS)]
    m_ref[...] = jnp.full(m_ref.shape, NEG, F32)

    def pass1(n, carry):
        _score_block(n, qh, kb_ref, lambda h: bias_ref[n], s_ref, m_ref)
        return carry

    lax.fori_loop(0, i + 1, pass1, 0)
    m_rows = [jnp.max(m_ref[h], axis=-1, keepdims=True) for h in range(N_HEADS)]
    l_ref[...] = jnp.zeros_like(l_ref)
    acc_ref[...] = jnp.zeros_like(acc_ref)

    def pass2(n, carry):
        _value_block(n, m_rows, vb_ref, s_ref, l_ref, acc_ref)
        return carry

    lax.fori_loop(0, i + 1, pass2, 0)
    o_ref[...] = _finish_heads(l_ref, acc_ref) * _silu(g_ref[...])


def _dsa_prompt(q, kt, vt, g, qi, wi, kit, *, B, L, name):
    T = MOBA_BLOCK
    nb = L // T
    n_sel = max(1, min(DSA_TOPK, L // 4))
    tok = lambda w: pl.BlockSpec((T, w), lambda b, i: (b * nb + i, 0))
    seq = lambda w: pl.BlockSpec((None, w, L), lambda b, i: (b, 0, 0))
    return pl.pallas_call(
        functools.partial(_dsa_prompt_body, L=L, nb=nb, n_sel=n_sel),
        grid=(B, nb),
        in_specs=[tok(256), seq(256), seq(256), tok(256), tok(512), tok(128), seq(64)],
        out_specs=tok(256),
        out_shape=jax.ShapeDtypeStruct((B * L, 256), F32),
        scratch_shapes=[pltpu.VMEM((nb, 256, T), BF16), pltpu.VMEM((nb, 256, T), BF16), pltpu.VMEM((nb, 64, T), BF16),
                        pltpu.VMEM((nb, T, T), jnp.int32), pltpu.VMEM((nb, T, T), F32),
                        pltpu.VMEM((T, LANES), jnp.int32), pltpu.VMEM((T, LANES), F32), pltpu.VMEM((T, LANES), F32)]
        + _attn_scratch(T, nb),
        compiler_params=_params(2),
        name=name,
    )(q, kt, vt, g, qi, wi, kit)


def _stack_heads(q):
    return jnp.concatenate([jnp.where(_head_cols(h), q, 0.0) for h in range(N_HEADS)], axis=0)


def _unstack_heads(o):
    out = jnp.zeros((SAMPLE_PAD, 256), F32)
    for h in range(N_HEADS):
        out = out + jnp.where(_head_cols(h), o[h * SAMPLE_PAD:(h + 1) * SAMPLE_PAD, :], 0.0)
    return out


def _page_scores(qs, pages):
    return jnp.concatenate([_nn(qs, _b(p[...])) for p in pages], axis=1)


def _page_values(p, pages, acc):
    for r, page in enumerate(pages):
        acc = acc + _nt(_b(p[:, r * LANES:(r + 1) * LANES]), _b(page[...]))
    return acc


def _own_scores(qs, kn_ref):
    R = qs.shape[0]
    s = _nt(qs, _b(kn_ref[...]))
    qrow = _iota((R, SAMPLE_PAD), 0) & (SAMPLE_PAD - 1)
    return jnp.where(_iota((R, SAMPLE_PAD), 1) <= qrow, s, NEG)


def _moba_sample_body(pt_ref, *refs, pps, n_ch, nblk):
    kp = refs[0:pps]
    vp = refs[pps:2 * pps]
    q_ref, kn_ref, vn_ref, g_ref, o_ref, qs_ref, gate_ref, s_ref, l_ref, acc_ref = refs[2 * pps:]
    j = pl.program_id(1)
    R = N_HEADS * SAMPLE_PAD
    W = pps * LANES
    lane = _iota((R, LANES), 1)

    @pl.when(j == 0)
    def _():
        qs_ref[...] = _b(_stack_heads(q_ref[...]))
        gate_ref[...] = jnp.zeros_like(gate_ref)

    @pl.when(j < n_ch)
    def _():
        s = _page_scores(qs_ref[...], kp)
        s_ref[j] = s
        for r in range(W // MOBA_BLOCK):
            n = j * (W // MOBA_BLOCK) + r
            g = jnp.sum(s[:, r * MOBA_BLOCK:(r + 1) * MOBA_BLOCK], axis=-1, keepdims=True) * (1.0 / MOBA_BLOCK)
            gate_ref[...] = jnp.where((lane & (nblk - 1)) == n, g, gate_ref[...])

    @pl.when(j == n_ch - 1)
    def _():
        gate = gate_ref[...]
        rank = jnp.zeros((R, LANES), jnp.int32)
        for d in range(1, nblk):
            other = pltpu.roll(gate, LANES - d, 1)
            wraps = (lane & (nblk - 1)) + d >= nblk
            rank = rank + jnp.where((other > gate) | ((other == gate) & wraps), 1, 0)
        sel = jnp.where((rank < min(MOBA_TOPK, nblk + 1)) & (lane < nblk), 1.0, 0.0).astype(BF16)
        s_own = _own_scores(qs_ref[...], kn_ref)
        m_all = jnp.max(s_own, axis=-1, keepdims=True)
        for c in range(n_ch):
            expand = jnp.where(((c * W + _iota((LANES, W), 1)) >> 8) == _iota((LANES, W), 0), 1.0, 0.0).astype(BF16)
            sm = jnp.where(_nn(sel, expand) > 0.5, s_ref[c], NEG)
            s_ref[c] = sm
            m_all = jnp.maximum(m_all, jnp.max(sm, axis=-1, keepdims=True))
        p_own = jnp.exp(s_own - m_all)
        l = jnp.sum(p_own, axis=-1, keepdims=True)
        for c in range(n_ch):
            p = jnp.exp(s_ref[c] - m_all)
            s_ref[c] = p
            l = l + jnp.sum(p, axis=-1, keepdims=True)
        l_ref[...] = jnp.broadcast_to(l, l_ref.shape)
        acc_ref[...] = _nn(_b(p_own), _b(vn_ref[...]))

    @pl.when(j >= n_ch)
    def _():
        acc_ref[...] = _page_values(s_ref[j - n_ch], vp, acc_ref[...])

    @pl.when(j == 2 * n_ch - 1)
    def _():
        o = acc_ref[...] * (1.0 / l_ref[:, 0:1])
        o_ref[...] = _unstack_heads(o) * _silu(g_ref[...])


def _page_specs(layer, pps, rows, phase, n_ch):
    def spec(r):
        def imap(b, j, pt):
            c = jnp.clip(j - phase * n_ch, 0, n_ch - 1)
            return (layer, pt[b, c * pps + r], 0, 0)
        return pl.BlockSpec((None, None, rows, LANES), imap)
    return [spec(r) for r in range(pps)]


def _moba_sample(page_table, kt_pool, vt_pool, layer, q, kn, vn, g, *, B, name):
    n_pages = page_table.shape[1]
    nblk = n_pages // 2
    assert nblk * 2 == n_pages and LANES % nblk == 0
    pps = min(SAMPLE_PAGES_PER_STEP, n_pages)
    n_ch = n_pages // pps
    assert n_ch * pps == n_pages and pps % 2 == 0
    R = N_HEADS * SAMPLE_PAD
    tok = pl.BlockSpec((SAMPLE_PAD, 256), lambda b, j, pt: (b, 0))
    grid_spec = pltpu.PrefetchScalarGridSpec(
        num_scalar_prefetch=1,
        grid=(B, 2 * n_ch),
        in_specs=_page_specs(layer, pps, 256, 0, n_ch) + _page_specs(layer, pps, 256, 1, n_ch) + [tok, tok, tok, tok],
        out_specs=tok,
        scratch_shapes=[pltpu.VMEM((R, 256), BF16), pltpu.VMEM((R, LANES), F32),
                        pltpu.VMEM((n_ch, R, pps * LANES), F32), pltpu.VMEM((R, LANES), F32),
                        pltpu.VMEM((R, 256), F32)],
    )
    return pl.pallas_call(
        functools.partial(_moba_sample_body, pps=pps, n_ch=n_ch, nblk=nblk),
        grid_spec=grid_spec,
        out_shape=jax.ShapeDtypeStruct((B * SAMPLE_PAD, 256), F32),
        compiler_params=_params(2),
        name=name,
    )(page_table, *([kt_pool] * pps), *([vt_pool] * pps), q, kn, vn, g)


def _dsa_sample_body(pt_ref, *refs, pps, n_ch, n_sel, past):
    ikp = refs[0:pps]
    kp = refs[pps:2 * pps]
    vp = refs[2 * pps:3 * pps]
    (q_ref, kn_ref, vn_ref, g_ref, qi_ref, wi_ref, kin_ref, o_ref,
     qs_ref, qis_ref, wis_ref, sc_ref, scn_ref, bias_ref, biasn_ref, s_ref, l_ref, acc_ref) = refs[3 * pps:]
    j = pl.program_id(1)
    R = N_HEADS * SAMPLE_PAD
    RI = DSA_IDX_HEADS * SAMPLE_PAD
    W = pps * LANES

    def index_scores(a):
        a = jnp.maximum(a, 0.0) * wis_ref[:, 0:1]
        out = a[0:SAMPLE_PAD, :]
        for h in range(1, DSA_IDX_HEADS):
            out = out + a[h * SAMPLE_PAD:(h + 1) * SAMPLE_PAD, :]
        return out

    @pl.when(j == 0)
    def _():
        qs_ref[...] = _b(_stack_heads(q_ref[...]))
        qi = qi_ref[...]
        wi = wi_ref[...]
        qis_ref[...] = _b(jnp.concatenate([qi[:, h * 64:(h + 1) * 64] for h in range(DSA_IDX_HEADS)], axis=0))
        wcol = jnp.concatenate([wi[:, h:h + 1] for h in range(DSA_IDX_HEADS)], axis=0)
        wis_ref[...] = jnp.broadcast_to(wcol, (RI, LANES))

    @pl.when(j < n_ch)
    def _():
        sc_ref[j] = index_scores(_page_scores(qis_ref[...], ikp))

    @pl.when(j == n_ch - 1)
    def _():
        s_new = index_scores(_nt(qis_ref[...], _b(kin_ref[...])))
        ok = _iota((SAMPLE_PAD, SAMPLE_PAD), 1) <= _iota((SAMPLE_PAD, SAMPLE_PAD), 0)
        s_new = jnp.where(ok, s_new, -jnp.inf)
        scn_ref[...] = jnp.concatenate([s_new, jnp.full((SAMPLE_PAD, LANES - SAMPLE_PAD), -jnp.inf, F32)], axis=1)
        keys = _sortable_keys(sc_ref[...])
        keys_new = _sortable_keys(scn_ref[...])

        def count(pred):
            past_cnt = jnp.sum(jnp.sum(jnp.where(pred(keys), 1, 0), axis=0), axis=-1, keepdims=True)
            return past_cnt + jnp.sum(jnp.where(pred(keys_new), 1, 0), axis=-1, keepdims=True)

        thr = _kth_largest(n_sel, lambda t: count(lambda kk: kk >= t), jnp.zeros((SAMPLE_PAD, 1), jnp.int32))
        cnt_ge = count(lambda kk: kk >= thr)
        cnt_gt = count(lambda kk: kk > thr)
        few = (past + _iota((SAMPLE_PAD, 1), 0) + 1) <= n_sel
        neg_inf_key = jnp.int32(_sortable_neg_inf())
        thr_eff = jnp.where(few, neg_inf_key + 1, thr)
        bias_ref[...] = jnp.where(keys >= thr_eff, 0.0, NEG)
        biasn_ref[...] = jnp.where(keys_new >= thr_eff, 0.0, NEG)
        tied = jnp.where(few, 0.0, jnp.where(cnt_ge > n_sel, 1.0, 0.0))

        @pl.when(jnp.max(tied) > 0.5)
        def _():
            need = (n_sel - cnt_gt).astype(F32)
            tri = jnp.where(_iota((LANES, LANES), 0) <= _iota((LANES, LANES), 1), 1.0, 0.0).astype(BF16)

            def tie_lanes(kc, seen):
                eq = jnp.where(kc == thr, 1.0, 0.0)
                upto = _nn(_b(eq), tri) + seen
                pick = jnp.where(kc > thr, 1.0, jnp.where(upto <= need, eq, 0.0))
                tie_bias = jnp.where(pick > 0.5, 0.0, NEG)
                bias = jnp.where(few, jnp.where(kc > neg_inf_key, 0.0, NEG), tie_bias)
                return bias, seen + jnp.sum(eq, axis=-1, keepdims=True)

            def tie_chunk(c, seen):
                kch = _sortable_keys(sc_ref[c])
                for r in range(pps):
                    cs = slice(r * LANES, (r + 1) * LANES)
                    bias, seen = tie_lanes(kch[:, cs], seen)
                    bias_ref[c, :, cs] = bias
                return seen

            seen = lax.fori_loop(0, n_ch, tie_chunk, jnp.zeros((SAMPLE_PAD, 1), F32))
            bias, _ = tie_lanes(keys_new, seen)
            biasn_ref[...] = bias

    @pl.when((j >= n_ch) & (j < 2 * n_ch))
    def _():
        c = j - n_ch
        s_ref[c] = _page_scores(qs_ref[...], kp) + jnp.concatenate([bias_ref[c]] * N_HEADS, axis=0)

    @pl.when(j == 2 * n_ch - 1)
    def _():
        s_own = _own_scores(qs_ref[...], kn_ref) + jnp.concatenate([biasn_ref[:, 0:SAMPLE_PAD]] * N_HEADS, axis=0)
        m_all = jnp.max(s_own, axis=-1, keepdims=True)
        for c in range(n_ch):
            m_all = jnp.maximum(m_all, jnp.max(s_ref[c], axis=-1, keepdims=True))
        p_own = jnp.exp(s_own - m_all)
        l = jnp.sum(p_own, axis=-1, keepdims=True)
        for c in range(n_ch):
            p = jnp.exp(s_ref[c] - m_all)
            s_ref[c] = p
            l = l + jnp.sum(p, axis=-1, keepdims=True)
        l_ref[...] = jnp.broadcast_to(l, l_ref.shape)
        acc_ref[...] = _nn(_b(p_own), _b(vn_ref[...]))

    @pl.when(j >= 2 * n_ch)
    def _():
        acc_ref[...] = _page_values(s_ref[j - 2 * n_ch], vp, acc_ref[...])

    @pl.when(j == 3 * n_ch - 1)
    def _():
        o = acc_ref[...] * (1.0 / l_ref[:, 0:1])
        o_ref[...] = _unstack_heads(o) * _silu(g_ref[...])


def _dsa_sample(page_table, kt_pool, vt_pool, ikt_pool, layer, q, kn, vn, g, qi, wi, kin, *, B, valid, name):
    n_pages = page_table.shape[1]
    past = n_pages * LANES
    pps = min(SAMPLE_PAGES_PER_STEP, n_pages)
    n_ch = n_pages // pps
    assert n_ch * pps == n_pages
    n_sel = max(1, min(DSA_TOPK, (past + valid) // 4))
    R = N_HEADS * SAMPLE_PAD
    RI = DSA_IDX_HEADS * SAMPLE_PAD
    W = pps * LANES
    tok = lambda w: pl.BlockSpec((SAMPLE_PAD, w), lambda b, j, pt: (b, 0))
    grid_spec = pltpu.PrefetchScalarGridSpec(
        num_scalar_prefetch=1,
        grid=(B, 3 * n_ch),
        in_specs=_page_specs(layer, pps, 64, 0, n_ch) + _page_specs(layer, pps, 256, 1, n_ch)
        + _page_specs(layer, pps, 256, 2, n_ch)
        + [tok(256), tok(256), tok(256), tok(256), tok(512), tok(128), tok(64)],
        out_specs=tok(256),
        scratch_shapes=[pltpu.VMEM((R, 256), BF16), pltpu.VMEM((RI, 64), BF16), pltpu.VMEM((RI, LANES), F32),
                        pltpu.VMEM((n_ch, SAMPLE_PAD, W), F32), pltpu.VMEM((SAMPLE_PAD, LANES), F32),
                        pltpu.VMEM((n_ch, SAMPLE_PAD, W), F32), pltpu.VMEM((SAMPLE_PAD, LANES), F32),
                        pltpu.VMEM((n_ch, R, W), F32), pltpu.VMEM((R, LANES), F32), pltpu.VMEM((R, 256), F32)],
    )
    return pl.pallas_call(
        functools.partial(_dsa_sample_body, pps=pps, n_ch=n_ch, n_sel=n_sel, past=past),
        grid_spec=grid_spec,
        out_shape=jax.ShapeDtypeStruct((B * SAMPLE_PAD, 256), F32),
        compiler_params=_params(2),
        name=name,
    )(page_table, *([ikt_pool] * pps), *([kt_pool] * pps), *([vt_pool] * pps), q, kn, vn, g, qi, wi, kin)


def _out_proj_body(a_ref, b_ref, c_ref, d_ref, x_ref, wo_ref, g1_ref, g2_ref, wq_ref, x1_ref, q_ref):
    acc = _nn(_b(a_ref[...]), wo_ref[0:256, :])
    acc = acc + _nn(_b(b_ref[...]), wo_ref[256:512, :])
    acc = acc + _nn(_b(c_ref[...]), wo_ref[512:768, :])
    acc = acc + _nn(_b(d_ref[...]), wo_ref[768:1024, :])
    x1 = x_ref[...] + _rms(acc, g1_ref[...])
    x1_ref[...] = x1
    q_ref[...] = _nn(_b(_rms(x1, g2_ref[...])), wq_ref[...]) * (XA_DIM ** -0.5)


def _out_proj(parts, x, wo, g1, g2, wq, *, name):
    n, d = x.shape
    tm = min(256, n)
    tok = lambda w: pl.BlockSpec((tm, w), lambda i: (i, 0))
    const = lambda a: pl.BlockSpec(a.shape, lambda i: (0,) * a.ndim)
    return pl.pallas_call(
        _out_proj_body,
        grid=(n // tm,),
        in_specs=[tok(256)] * 4 + [tok(d), const(wo), const(g1), const(g2), const(wq)],
        out_specs=[tok(d), tok(d)],
        out_shape=[jax.ShapeDtypeStruct((n, d), F32)] * 2,
        compiler_params=_params(1),
        name=name,
    )(*parts, x, wo, g1, g2, wq)


def _xattn_body(q_ref, mk_ref, mv_ref, wo_ref, g_ref, x_ref, o_ref):
    acc = jnp.zeros(x_ref.shape, F32)
    for h in range(N_HEADS):
        hs = slice(h * XA_DIM, (h + 1) * XA_DIM)
        s = _nt(_b(q_ref[:, hs]), _b(mk_ref[:, h, :]))
        m = jnp.max(s, axis=-1, keepdims=True)
        p = jnp.exp(s - m)
        oh = _nn(_b(p), _b(mv_ref[:, h, :])) * (1.0 / jnp.sum(p, axis=-1, keepdims=True))
        acc = acc + _nn(_b(oh), wo_ref[hs, :])
    o_ref[...] = x_ref[...] + _rms(acc, g_ref[...])


def _xattn(q, mem_k, mem_v, layer, wo, g, x, *, B, L, name):
    T = min(256, L)
    nt = L // T
    d = x.shape[1]
    m = mem_k.shape[2]
    tok = pl.BlockSpec((T, d), lambda b, t: (b * nt + t, 0))
    mem = pl.BlockSpec((None, None, m, N_HEADS, XA_DIM), lambda b, t: (layer, b, 0, 0, 0))
    const = lambda a: pl.BlockSpec(a.shape, lambda b, t: (0,) * a.ndim)
    return pl.pallas_call(
        _xattn_body,
        grid=(B, nt),
        in_specs=[tok, mem, mem, const(wo), const(g), tok],
        out_specs=tok,
        out_shape=jax.ShapeDtypeStruct(x.shape, F32),
        compiler_params=_params(2),
        name=name,
    )(q, mem_k, mem_v, wo, g, x)


def _pad_cols(w, width):
    return jnp.pad(w, ((0, 0), (0, width - w.shape[1])))


_IN_SEGS = (
    ("g_q", 0, 128, False, GLA_DK ** -0.5, 128), ("g_k", 1, 128, False, 1.0, 128), ("g_v", 2, 256, False, 1.0, 256),
    ("g_gate", 3, 256, False, 1.0, 256), ("g_lr", 4, 128, False, 1.0, 128),
    ("s_z", 5, 256, False, 1.0, 256), ("s_xbc", 6, 768, False, 1.0, 768), ("s_dt", 7, 128, False, 1.0, 128),
    ("m_q", 8, 256, True, HEAD_DIM ** -0.5, 256), ("m_k", 9, 256, True, 1.0, 256), ("m_v", 10, 256, False, 1.0, 256),
    ("m_gate", 11, 256, False, 1.0, 256),
    ("d_q", 12, 256, True, HEAD_DIM ** -0.5, 256), ("d_k", 13, 256, True, 1.0, 256), ("d_v", 14, 256, False, 1.0, 256),
    ("d_gate", 15, 256, False, 1.0, 256), ("d_qi", 16, 512, True, HEAD_DIM ** -0.5, 512),
    ("d_ki", 17, 128, True, 1.0, 64), ("d_wi", 18, 128, False, DSA_IDX_HEADS ** -0.5, 128),
)
_KV_SEGS = (("m_k", 9, 256, True), ("m_v", 10, 256, False), ("d_k", 13, 256, True), ("d_v", 14, 256, False),
            ("d_ki", 17, 64, True))


def _in_proj_layout(w_in_l, transposed_kv):
    offs = np.concatenate([[0], np.cumsum(IN_SPLITS)])
    kv_names = {s[0] for s in _KV_SEGS} if transposed_kv else set()
    cols, segs, names, c0 = [], [], [], 0
    for name, idx, width, rope, scale, ow in _IN_SEGS:
        if name in kv_names:
            continue
        cols.append(_pad_cols(w_in_l[:, offs[idx]:offs[idx + 1]], width))
        segs.append((c0, width, rope, scale, ow))
        names.append(name)
        c0 += width
    w_cat = _b(jnp.concatenate(cols, axis=1))
    if not transposed_kv:
        return w_cat, tuple(segs), names, None, (), []
    rows, tsegs, tnames, r0 = [], [], [], 0
    for name, idx, width, rope in _KV_SEGS:
        rows.append(w_in_l[:, offs[idx]:offs[idx + 1]].T)
        tsegs.append((r0, width, rope))
        tnames.append(name)
        r0 += width
    return w_cat, tuple(segs), names, _b(jnp.concatenate(rows, axis=0)), tuple(tsegs), tnames


def _rope_tables(pos, rows):
    half = HEAD_DIM // 2
    inv = ROPE_THETA ** (-jnp.arange(half, dtype=F32) / half)
    ang = pos.astype(F32)[:, None] * inv[None, :]
    cos, sin = jnp.cos(ang), jnp.sin(ang)
    cs1 = jnp.concatenate([cos, cos], axis=1)
    sn1 = jnp.concatenate([-sin, sin], axis=1)
    cs = jnp.tile(cs1, (1, 512 // HEAD_DIM))
    sn = jnp.tile(sn1, (1, 512 // HEAD_DIM))
    if cs.shape[0] < rows:
        cs = jnp.tile(cs, (rows // cs.shape[0], 1))
        sn = jnp.tile(sn, (rows // sn.shape[0], 1))
    cst = jnp.tile(cs1.T, (256 // HEAD_DIM, 1))
    snt = jnp.tile(sn1.T, (256 // HEAD_DIM, 1))
    return (cs, sn), (cst, snt)


def _head_rep(v4, widths):
    return jnp.concatenate([jnp.repeat(v4, w) for w in widths]).reshape(1, -1)


def _layer_consts(l, W):
    e = np.zeros((128, 768), np.float32)
    for h in range(N_HEADS):
        e[h, h * 64:(h + 1) * 64] = 1.0
        e[h, 256 + h * 128:256 + (h + 1) * 128] = 1.0
    ssd_consts = (jnp.asarray(e),
                  jnp.pad(W["ssd_conv_w"][l], ((0, 8 - SSD_CONV), (0, 0))),
                  W["ssd_conv_b"][l].reshape(1, -1),
                  _head_rep(W["ssd_dt_bias"][l], (64, 128)),
                  _head_rep(W["ssd_a_log"][l], (64, 128)),
                  _head_rep(W["ssd_d"][l], (64,)),
                  W["ssd_norm"][l].reshape(1, -1))
    return dict(
        proj_p=_in_proj_layout(W["w_in"][l], True), proj_s=_in_proj_layout(W["w_in"][l], False),
        wlr=jnp.pad(W["gla_w_lr"][l], ((0, 128 - GLA_RANK), (0, 0))),
        blr=W["gla_b_lr"][l].reshape(1, -1),
        gn=jnp.tile(W["gla_norm"][l], N_HEADS).reshape(1, -1),
        ssd=ssd_consts,
        wo=_b(W["w_out"][l]), wq=_b(W["xa_wq"][l]), xwo=_b(W["xa_wo"][l]),
        wkv=_b(jnp.concatenate([W["xa_wk"][l], W["xa_wv"][l]], axis=1)),
    )


def _gla_state_to_t(s):
    B = s.shape[0]
    out = jnp.zeros((B, N_HEADS, 64, N_HEADS, GLA_DK), F32)
    for h in range(N_HEADS):
        out = out.at[:, h, :, h, :].set(jnp.swapaxes(s[:, h], 1, 2))
    return out.reshape(B, 256, 128)


def _gla_state_from_t(st):
    B = st.shape[0]
    s5 = st.reshape(B, N_HEADS, 64, N_HEADS, GLA_DK)
    return jnp.stack([jnp.swapaxes(s5[:, h, :, h, :], 1, 2) for h in range(N_HEADS)], axis=1)


def _mixer_layer(x, l, C, W, tabs, *, B, L, valid, gla_s0t, ssd_s0, ssd_s0_idx, conv0, mem_k, mem_v, mem_idx,
                 sample_ctx, tag):
    rope, rope_t = tabs
    if sample_ctx is None:
        w_cat, segs, names, wt_cat, tsegs, tnames = C["proj_p"]
        outs = _norm_proj(x, W["norm_mix_pre"][l], w_cat, segs, rope, wt_cat, tsegs, rope_t, L, name=f"in_proj_{tag}")
    else:
        w_cat, segs, names, wt_cat, tsegs, tnames = C["proj_s"]
        outs = _norm_proj(x, W["norm_mix_pre"][l], w_cat, segs, rope, name=f"in_proj_{tag}")
    P = dict(zip(names + tnames, outs))
    o_gla, gla_st = _gla(P["g_q"], P["g_k"], P["g_v"], P["g_gate"], P["g_lr"], C["wlr"], C["blr"], C["gn"], gla_s0t,
                         B=B, L=L, valid=valid, name=f"gla_{tag}")
    o_ssd, ssd_s, conv_new = _ssd(P["s_z"], P["s_xbc"], P["s_dt"], conv0, C["ssd"], ssd_s0, ssd_s0_idx,
                                  B=B, L=L, valid=valid, name=f"ssd_{tag}")
    if sample_ctx is None:
        o_moba = _moba_prompt(P["m_q"], P["m_k"], P["m_v"], P["m_gate"], B=B, L=L, name=f"moba_{tag}")
        o_dsa = _dsa_prompt(P["d_q"], P["d_k"], P["d_v"], P["d_gate"], P["d_qi"], P["d_wi"], P["d_ki"],
                            B=B, L=L, name=f"dsa_{tag}")
    else:
        pt, mk_pool, mv_pool, dk_pool, dv_pool, ik_pool = sample_ctx
        o_moba = _moba_sample(pt, mk_pool, mv_pool, l, P["m_q"], P["m_k"], P["m_v"], P["m_gate"], B=B,
                              name=f"moba_{tag}")
        o_dsa = _dsa_sample(pt, dk_pool, dv_pool, ik_pool, l, P["d_q"], P["d_k"], P["d_v"], P["d_gate"],
                            P["d_qi"], P["d_wi"], P["d_ki"], B=B, valid=valid, name=f"dsa_{tag}")
    x1, xq = _out_proj((o_gla, o_ssd, o_moba, o_dsa), x, C["wo"], W["norm_mix_post"][l].reshape(1, -1),
                       W["norm_xa_pre"][l].reshape(1, -1), C["wq"], name=f"out_proj_{tag}")
    x2 = _xattn(xq, mem_k, mem_v, mem_idx, C["xwo"], W["norm_xa_post"][l].reshape(1, -1), x1, B=B, L=L,
                name=f"xattn_{tag}")
    new = (gla_st, ssd_s, conv_new, P["m_k"], P["m_v"], P["d_k"], P["d_v"], P["d_ki"])
    return x2, new


def kernel(x_prompt, x_sample, state_gla, state_ssd, state_conv, cache_moba_k, cache_moba_v,
           cache_dsa_k, cache_dsa_v, cache_dsa_ik, cache_mem_k, cache_mem_v, page_table, mem_prompt,
           norm_mix_pre, norm_mix_post, norm_xa_pre, norm_xa_post, norm_mem, w_in, gla_w_lr, gla_b_lr,
           gla_norm, ssd_conv_w, ssd_conv_b, ssd_dt_bias, ssd_a_log, ssd_d, ssd_norm, w_out,
           xa_wq, xa_wk, xa_wv, xa_wo):
    W = dict(norm_mix_pre=norm_mix_pre, norm_mix_post=norm_mix_post, norm_xa_pre=norm_xa_pre,
             norm_xa_post=norm_xa_post, norm_mem=norm_mem, w_in=w_in, gla_w_lr=gla_w_lr, gla_b_lr=gla_b_lr,
             gla_norm=gla_norm, ssd_conv_w=ssd_conv_w, ssd_conv_b=ssd_conv_b, ssd_dt_bias=ssd_dt_bias,
             ssd_a_log=ssd_a_log, ssd_d=ssd_d, ssd_norm=ssd_norm, w_out=w_out, xa_wq=xa_wq, xa_wk=xa_wk,
             xa_wv=xa_wv, xa_wo=xa_wo)
    depth = w_in.shape[0]
    Bp, S, D = x_prompt.shape
    Bs, Ls, _ = x_sample.shape
    M = mem_prompt.shape[1]
    n_pool, page = cache_moba_k.shape[1], cache_moba_k.shape[2]
    assert page == LANES and Ls <= SAMPLE_PAD and S % MOBA_BLOCK == 0
    past = page_table.shape[1] * page
    consts = [_layer_consts(l, W) for l in range(depth)]

    tabs_p = _rope_tables(jnp.arange(S, dtype=jnp.int32), min(256, Bp * S))
    zero_gla = jnp.zeros((Bp, 256, 128), F32)
    zero_ssd = jnp.zeros((1, Bp, 256, 128), F32)
    zero_conv = jnp.zeros((Bp, 8, 768), F32)
    mem_flat = mem_prompt.reshape(Bp * M, D)
    kv_segs = ((0, D, False, 1.0, D), (D, D, False, 1.0, D))
    x = x_prompt.reshape(Bp * S, D)
    p_new = []
    for l in range(depth):
        C = consts[l]
        mk, mv = _norm_proj(mem_flat, norm_mem[l], C["wkv"], kv_segs, name="mem_kv")
        mk5 = mk.reshape(1, Bp, M, N_HEADS, XA_DIM)
        mv5 = mv.reshape(1, Bp, M, N_HEADS, XA_DIM)
        x, new = _mixer_layer(x, l, C, W, tabs_p, B=Bp, L=S, valid=S, gla_s0t=zero_gla, ssd_s0=zero_ssd,
                              ssd_s0_idx=0, conv0=zero_conv, mem_k=mk5, mem_v=mv5, mem_idx=0, sample_ctx=None, tag="p")
        p_new.append(new + (mk5[0], mv5[0]))
    y_prompt = x.reshape(Bp, S, D)
    (p_gla, p_ssd, p_conv, p_mk, p_mv, p_dk, p_dv, p_ik, p_memk, p_memv) = [jnp.stack(t) for t in zip(*p_new)]
    p_gla = _gla_state_from_t(p_gla.reshape(depth * Bp, 256, 128)).reshape(depth, Bp, N_HEADS, GLA_DK, 64)
    p_ssd = p_ssd.reshape(depth, Bp, N_HEADS, 64, SSD_N)
    p_conv = p_conv[:, :, :SSD_CONV - 1]
    kv5 = lambda a: jnp.transpose(a.reshape(depth, Bp, N_HEADS, HEAD_DIM, S), (0, 1, 4, 2, 3))
    p_ik = jnp.transpose(p_ik, (0, 1, 3, 2))

    Lp = SAMPLE_PAD
    tabs_s = _rope_tables(past + jnp.arange(Lp, dtype=jnp.int32), min(256, Bs * Lp))
    xs = jnp.pad(x_sample, ((0, 0), (0, Lp - Ls), (0, 0))).reshape(Bs * Lp, D)
    gla_s0_all = _gla_state_to_t(state_gla.reshape(depth * Bs, N_HEADS, GLA_DK, 64)).reshape(depth, Bs, 256, 128)
    ssd_s0_all = state_ssd.reshape(depth, Bs, 256, SSD_N)
    conv0_all = jnp.pad(state_conv, ((0, 0), (0, 0), (8 - (SSD_CONV - 1), 0), (0, 0)))
    kv_t = lambda a: jnp.transpose(a, (0, 1, 3, 4, 2)).reshape(depth, n_pool, N_HEADS * HEAD_DIM, page)
    pools = (kv_t(cache_moba_k), kv_t(cache_moba_v), kv_t(cache_dsa_k), kv_t(cache_dsa_v),
             jnp.transpose(cache_dsa_ik, (0, 1, 3, 2)))
    s_new = []
    for l in range(depth):
        xs, new = _mixer_layer(xs, l, consts[l], W, tabs_s, B=Bs, L=Lp, valid=Ls, gla_s0t=gla_s0_all[l],
                               ssd_s0=ssd_s0_all, ssd_s0_idx=l, conv0=conv0_all[l], mem_k=cache_mem_k,
                               mem_v=cache_mem_v, mem_idx=l, sample_ctx=(page_table,) + pools, tag="s")
        s_new.append(new)
    y_sample = xs.reshape(Bs, Lp, D)[:, :Ls]
    (s_gla, s_ssd, s_conv, s_mk, s_mv, s_dk, s_dv, s_ik) = [jnp.stack(t) for t in zip(*s_new)]
    s_gla = _gla_state_from_t(s_gla.reshape(depth * Bs, 256, 128)).reshape(depth, Bs, N_HEADS, GLA_DK, 64)
    s_ssd = s_ssd.reshape(depth, Bs, N_HEADS, 64, SSD_N)
    s_conv = s_conv[:, :, :SSD_CONV - 1]
    skv5 = lambda a: a.reshape(depth, Bs, Lp, N_HEADS, HEAD_DIM)[:, :, :Ls]

    return (y_prompt, y_sample, p_gla, p_ssd, p_conv, kv5(p_mk), kv5(p_mv), kv5(p_dk), kv5(p_dv),
            p_ik, p_memk, p_memv,
            s_gla, s_ssd, s_conv, skv5(s_mk), skv5(s_mv), skv5(s_dk), skv5(s_dv),
            s_ik.reshape(depth, Bs, Lp, HEAD_DIM)[:, :, :Ls])
```

```python
import functools

import jax
import jax.numpy as jnp
import numpy as np
from jax import lax
from jax.experimental import pallas as pl
from jax.experimental.pallas import tpu as pltpu

F32 = jnp.float32
BF16 = jnp.bfloat16
HI = lax.Precision.HIGHEST

D_MODEL = 1024
HEAD_DIM = 64
ROPE_THETA = 10000.0
EPS = 1e-6
N_HEADS = 4
GLA_DK = 32
GLA_RANK = 16
GLA_TAU = 16.0
SCAN_CHUNK = 64
SSD_N = 128
SSD_CONV = 4
MOBA_BLOCK = 256
MOBA_TOPK = 3
DSA_IDX_HEADS = 8
DSA_TOPK = 256
XA_DIM = 256
IN_SPLITS = (128, 128, 256, 256, 16, 256, 768, 4, 256, 256, 256, 256, 256, 256, 256, 256, 512, 64, 8)

LANES = 128
SUBLANES = 8
SAMPLE_PAD = SUBLANES
SAMPLE_PAGES_PER_STEP = 16
VMEM_LIMIT = 56 * 1024 * 1024
NEG = -1e30
_INT_MIN = -2 ** 31


def _nn(a, b, precision=None):
    return jnp.dot(a, b, preferred_element_type=F32, precision=precision)


def _nt(a, b, precision=None):
    return lax.dot_general(a, b, (((1,), (1,)), ((), ())), preferred_element_type=F32, precision=precision)


def _tn(a, b, precision=None):
    return lax.dot_general(a, b, (((0,), (0,)), ((), ())), preferred_element_type=F32, precision=precision)


def _b(x):
    return x.astype(BF16)


def _sigmoid(x):
    return 1.0 / (1.0 + jnp.exp(-x))


def _silu(x):
    return x * _sigmoid(x)


def _softplus(x):
    return jnp.maximum(x, 0.0) + jnp.log1p(jnp.exp(-jnp.abs(x)))


def _log_sigmoid(x):
    return jnp.minimum(x, 0.0) - jnp.log1p(jnp.exp(-jnp.abs(x)))


def _rms(x, g):
    return x * lax.rsqrt(jnp.mean(x * x, axis=-1, keepdims=True) + EPS) * g


def _iota(shape, dim):
    return lax.broadcasted_iota(jnp.int32, shape, dim)


def _fold(x):
    return x[:, 0:LANES], x[:, LANES:2 * LANES]


def _params(n_axes):
    return pltpu.CompilerParams(dimension_semantics=("arbitrary",) * n_axes, vmem_limit_bytes=VMEM_LIMIT)


def _norm_proj_body(*refs, segs, tsegs, has_rope):
    refs = list(refs)
    x_ref, g_ref, w_ref = refs[:3]
    del refs[:3]
    cs_ref, sn_ref = (refs.pop(0), refs.pop(0)) if has_rope else (None, None)
    cst_ref, snt_ref = (refs.pop(0), refs.pop(0)) if (has_rope and tsegs) else (None, None)
    y_ref = refs.pop()
    outs = refs
    y_ref[...] = _nn(_b(_rms(x_ref[...], g_ref[...])), w_ref[...])
    for (c0, w, rope, scale, ow), o_ref in zip(segs, outs):
        if ow > w:
            y = jnp.where(_iota((1, ow), 1) < w, y_ref[:, c0:c0 + ow], 0.0)
        else:
            y = y_ref[:, c0:c0 + w]
        if rope and w % LANES == 0:
            lane = _iota(y.shape, 1)
            sw = jnp.where((lane & 63) < 32, pltpu.roll(y, w - 32, 1), pltpu.roll(y, 32, 1))
            y = y * cs_ref[:, :w] + sw * sn_ref[:, :w]
        elif rope:
            halves = []
            for h in range(w // HEAD_DIM):
                halves += [y[:, h * 64 + 32:h * 64 + 64], y[:, h * 64:h * 64 + 32]]
            y = y * cs_ref[:, :w] + jnp.concatenate(halves, axis=1) * sn_ref[:, :w]
        if scale != 1.0:
            y = y * scale
        o_ref[...] = y
    for (c0, w, rope), o_ref in zip(tsegs, outs[len(segs):]):
        yt = y_ref[:, c0:c0 + w].T
        if rope:
            halves = []
            for h in range(w // HEAD_DIM):
                halves += [yt[h * 64 + 32:h * 64 + 64, :], yt[h * 64:h * 64 + 32, :]]
            yt = yt * cst_ref[0:w, :] + jnp.concatenate(halves, axis=0) * snt_ref[0:w, :]
        o_ref[...] = yt


def _norm_proj(x, gain, w_cat, segs, rope_tabs=None, tsegs=(), rope_tabs_t=None, seq_len=None, *, name):
    n, d = x.shape
    tm = min(256, n)
    has_rope = rope_tabs is not None
    in_specs = [pl.BlockSpec((tm, d), lambda i: (i, 0)),
                pl.BlockSpec((1, d), lambda i: (0, 0)),
                pl.BlockSpec(w_cat.shape, lambda i: (0, 0))]
    args = [x, gain.reshape(1, d), w_cat]
    if has_rope:
        cs, sn = rope_tabs
        nt = cs.shape[0] // tm
        in_specs += [pl.BlockSpec((tm, cs.shape[1]), lambda i: (i % nt, 0))] * 2
        args += [cs, sn]
    out_specs = [pl.BlockSpec((tm, s[4]), lambda i: (i, 0)) for s in segs]
    out_shape = [jax.ShapeDtypeStruct((n, s[4]), F32) for s in segs]
    if tsegs:
        nts = seq_len // tm
        assert nts * tm == seq_len
        if has_rope:
            cst, snt = rope_tabs_t
            in_specs += [pl.BlockSpec((cst.shape[0], tm), lambda i: (0, i % nts))] * 2
            args += [cst, snt]
        out_specs += [pl.BlockSpec((None, s[1], tm), lambda i: (i // nts, 0, i % nts)) for s in tsegs]
        out_shape += [jax.ShapeDtypeStruct((n // seq_len, s[1], seq_len), F32) for s in tsegs]
    return pl.pallas_call(
        functools.partial(_norm_proj_body, segs=segs, tsegs=tuple(tsegs), has_rope=has_rope),
        grid=(n // tm,),
        in_specs=in_specs,
        out_specs=out_specs,
        out_shape=out_shape,
        scratch_shapes=[pltpu.VMEM((tm, w_cat.shape[1]), F32)],
        compiler_params=_params(1),
        name=name,
    )(*args)


def _gla_body(q_ref, k_ref, v_ref, g_ref, lr_ref, wlr_ref, blr_ref, gn_ref, s0_ref, o_ref, so_ref, st_ref,
              *, T, C, valid, nt):
    t = pl.program_id(1)

    @pl.when(t == 0)
    def _():
        st_ref[...] = s0_ref[...]

    row = _iota((C, C), 0)
    col = _iota((C, C), 1)
    causal = col <= row
    tril = causal.astype(F32)
    lane_qk = _iota((1, 128), 1)
    lane_v = _iota((1, 256), 1)
    bd_state = (_iota((256, 128), 0) >> 6) == (_iota((256, 128), 1) >> 5)
    bd_head = ((_iota((256, 256), 0) >> 6) == (_iota((256, 256), 1) >> 6)).astype(F32)
    st = st_ref[...]
    for c in range(T // C):
        sl = slice(c * C, (c + 1) * C)
        q = q_ref[sl, :]
        k = k_ref[sl, :]
        v = v_ref[sl, :]
        logit = _nn(lr_ref[sl, :], wlr_ref[...], HI) + blr_ref[...]
        logf = _log_sigmoid(logit) * (1.0 / GLA_TAU)
        if valid < T:
            live = _iota((C, 1), 0) < valid
            logf = jnp.where(live, logf, 0.0)
            k = jnp.where(live, k, 0.0)
        gcum = _nn(tril, logf, HI)
        gend = gcum[C - 1:C, :]
        qd = q * jnp.exp(gcum)
        kinv = k * jnp.exp(-gcum)
        kend = k * jnp.exp(gend - gcum)
        o = _nt(_b(qd), _b(st))
        for h in range(N_HEADS):
            att = _nt(_b(jnp.where((lane_qk >> 5) == h, qd, 0.0)), _b(kinv))
            att = jnp.where(causal, att, 0.0)
            o = o + _nn(_b(att), _b(jnp.where((lane_v >> 6) == h, v, 0.0)))
        st = st * jnp.exp(gend) + jnp.where(bd_state, _tn(_b(v), _b(kend)), 0.0)
        ms = _nn(o * o, bd_head, HI) * (1.0 / HEAD_DIM)
        o_ref[sl, :] = o * lax.rsqrt(ms + EPS) * gn_ref[...] * _silu(g_ref[sl, :])
    st_ref[...] = st

    @pl.when(t == nt - 1)
    def _():
        so_ref[...] = st


def _gla(q, k, v, g, lr, wlr, blr, gn, s0t, *, B, L, valid, name):
    T = min(256, L)
    C = min(SCAN_CHUNK, T)
    nt = L // T
    assert valid == L or nt == 1
    tok = lambda w: pl.BlockSpec((T, w), lambda b, t: (b * nt + t, 0))
    const = lambda a: pl.BlockSpec(a.shape, lambda b, t: (0,) * a.ndim)
    st_spec = pl.BlockSpec((None, 256, 128), lambda b, t: (b, 0, 0))
    return pl.pallas_call(
        functools.partial(_gla_body, T=T, C=C, valid=valid, nt=nt),
        grid=(B, nt),
        in_specs=[tok(128), tok(128), tok(256), tok(256), tok(128), const(wlr), const(blr), const(gn), st_spec],
        out_specs=[tok(256), st_spec],
        out_shape=[jax.ShapeDtypeStruct((B * L, 256), F32), jax.ShapeDtypeStruct((B, 256, 128), F32)],
        scratch_shapes=[pltpu.VMEM((256, 128), F32)],
        compiler_params=_params(2),
        name=name,
    )(q, k, v, g, lr, wlr, blr, gn, s0t)


def _ssd_body(z_ref, xbc_ref, dtr_ref, c0_ref, e_ref, cw_ref, cb_ref, dtb_ref, alog_ref, d_ref, nw_ref, s0_ref,
              o_ref, so_ref, cn_ref, s_ref, xb_ref, *, T, C, valid, nt):
    t = pl.program_id(1)

    @pl.when(t == 0)
    def _():
        s_ref[...] = s0_ref[...]
        xb_ref[0:8, :] = c0_ref[...]

    xb_ref[8:8 + T, :] = xbc_ref[...]
    conv = cb_ref[...] + xb_ref[5:5 + T, :] * cw_ref[0:1, :]
    for j in range(1, SSD_CONV):
        conv = conv + xb_ref[5 + j:5 + j + T, :] * cw_ref[j:j + 1, :]
    act = _silu(conv)
    dt = _softplus(_nn(dtr_ref[...], e_ref[...], HI) + dtb_ref[...])
    v_last = valid - (nt - 1) * T
    if v_last < T:
        dt = jnp.where(_iota((T, 1), 0) < v_last, dt, 0.0)
    lac = dt * (-jnp.exp(alog_ref[...]))

    row = _iota((C, C), 0)
    col = _iota((C, C), 1)
    causal = col <= row
    tril = causal.astype(F32)
    triu = (row <= col).astype(F32)
    lane_x = _iota((1, 256), 1)
    s_all = s_ref[...]
    for c in range(T // C):
        sl = slice(c * C, (c + 1) * C)
        x = act[sl, 0:256]
        bm = act[sl, 256:512]
        cm = act[sl, 512:768]
        xdt = x * dt[sl, 0:256]
        cum64 = _nn(tril, lac[sl, 0:256], HI)
        cum128 = _nn(tril, lac[sl, 256:768], HI)
        cum_t = _tn(lac[sl, 256:768], triu, HI)
        cb = [_nt(_b(cm[:, g * 128:(g + 1) * 128]), _b(bm[:, g * 128:(g + 1) * 128])) for g in range(2)]
        y = x * d_ref[...]
        for h in range(N_HEADS):
            seg = cum128[:, h * 128:h * 128 + C] - cum_t[h * 128:h * 128 + C, :]
            w = jnp.where(causal, cb[h // 2] * jnp.exp(jnp.where(causal, seg, 0.0)), 0.0)
            y = y + _nn(_b(w), _b(jnp.where((lane_x >> 6) == h, xdt, 0.0)))
        ys0 = _nt(_b(cm[:, 0:128]), _b(s_all))
        ys1 = _nt(_b(cm[:, 128:256]), _b(s_all))
        y = y + jnp.where(lane_x < 128, ys0, ys1) * jnp.exp(cum64)
        new_rows = []
        for h in range(N_HEADS):
            g = h // 2
            c_end = cum128[C - 1:C, h * 128:(h + 1) * 128]
            bw = bm[:, g * 128:(g + 1) * 128] * jnp.exp(c_end - cum128[:, h * 128:(h + 1) * 128])
            upd = _tn(_b(xdt[:, h * 64:(h + 1) * 64]), _b(bw))
            new_rows.append(s_all[h * 64:(h + 1) * 64, :] * jnp.exp(c_end) + upd)
        s_all = jnp.concatenate(new_rows, axis=0)
        o_ref[sl, :] = _rms(y * _silu(z_ref[sl, :]), nw_ref[...])
    s_ref[...] = s_all
    xb_ref[0:8, :] = xb_ref[T:T + 8, :]

    @pl.when(t == nt - 1)
    def _():
        so_ref[...] = s_all
        cn_ref[...] = jnp.zeros_like(cn_ref)
        cn_ref[0:SSD_CONV - 1, :] = xb_ref[8 + v_last - (SSD_CONV - 1):8 + v_last, :]


def _ssd(z, xbc, dtr, conv0, consts, s0, s0_index, *, B, L, valid, name):
    T = min(256, L)
    C = min(SCAN_CHUNK, T)
    nt = L // T
    assert valid == L or nt == 1
    tok = lambda w: pl.BlockSpec((T, w), lambda b, t: (b * nt + t, 0))
    const = lambda a: pl.BlockSpec(a.shape, lambda b, t: (0,) * a.ndim)
    l_idx = s0_index
    s0_spec = pl.BlockSpec((None, None, 256, 128), lambda b, t: (l_idx, b, 0, 0))
    so_spec = pl.BlockSpec((None, 256, 128), lambda b, t: (b, 0, 0))
    cv_spec = pl.BlockSpec((None, 8, 768), lambda b, t: (b, 0, 0))
    return pl.pallas_call(
        functools.partial(_ssd_body, T=T, C=C, valid=valid, nt=nt),
        grid=(B, nt),
        in_specs=[tok(256), tok(768), tok(128), cv_spec] + [const(a) for a in consts] + [s0_spec],
        out_specs=[tok(256), so_spec, cv_spec],
        out_shape=[jax.ShapeDtypeStruct((B * L, 256), F32), jax.ShapeDtypeStruct((B, 256, 128), F32),
                   jax.ShapeDtypeStruct((B, 8, 768), F32)],
        scratch_shapes=[pltpu.VMEM((256, 128), F32), pltpu.VMEM((T + 8, 768), F32)],
        compiler_params=_params(2),
        name=name,
    )(z, xbc, dtr, conv0, *consts, s0)


def _head_cols(h):
    return (_iota((1, 256), 1) >> 6) == h


def _score_block(n, qh, kb_ref, bias_of_head, s_ref, m_ref):
    kb = kb_ref[n]
    for h in range(N_HEADS):
        s = _nn(qh[h], kb[h * 64:(h + 1) * 64, :]) + bias_of_head(h)
        s_ref[h, n] = s
        lo, hi = _fold(s)
        m_ref[h] = jnp.maximum(m_ref[h], jnp.maximum(lo, hi))


def _value_block(n, m_rows, vb_ref, s_ref, l_ref, acc_ref):
    vb = vb_ref[n]
    tot = None
    for h in range(N_HEADS):
        p = jnp.exp(s_ref[h, n] - m_rows[h])
        lo, hi = _fold(p)
        l_ref[h] = l_ref[h] + lo + hi
        pv = jnp.where(_head_cols(h), _nt(_b(p), vb), 0.0)
        tot = pv if tot is None else tot + pv
    acc_ref[...] = acc_ref[...] + tot


def _finish_heads(l_ref, acc_ref):
    inv = jnp.zeros(acc_ref.shape, F32)
    for h in range(N_HEADS):
        inv = inv + jnp.where(_head_cols(h), 1.0 / jnp.sum(l_ref[h], axis=-1, keepdims=True), 0.0)
    return acc_ref[...] * inv


def _moba_prompt_body(q_ref, kt_ref, vt_ref, g_ref, o_ref, kb_ref, vb_ref, km_ref, s_ref, m_ref, l_ref, acc_ref,
                      *, L, nb):
    i = pl.program_id(1)
    T = MOBA_BLOCK
    nbp = km_ref.shape[0]

    @pl.when(i == 0)
    def _():
        for n in range(nb):
            kb_ref[n] = _b(kt_ref[:, n * T:(n + 1) * T])
            vb_ref[n] = _b(vt_ref[:, n * T:(n + 1) * T])
        mean_of_block = jnp.where((_iota((nbp, L), 1) >> 8) == _iota((nbp, L), 0), 1.0 / T, 0.0)
        km_ref[...] = _nt(mean_of_block, kt_ref[...], HI)

    q = q_ref[...]
    km = km_ref[...]
    blk = _iota((nbp, T), 0)
    n_sel = min(MOBA_TOPK, nb)
    sel_cols = []
    for h in range(N_HEADS):
        gt = _nt(jnp.where(_head_cols(h), km, 0.0), q, HI)
        rank = jnp.zeros((nbp, T), jnp.int32)
        for n2 in range(nb):
            other = gt[n2:n2 + 1, :]
            beats = (other > gt) | ((other == gt) & (n2 < blk))
            rank = rank + jnp.where(beats, 1, 0) * jnp.where(n2 < i, 1, 0)
        sel_cols.append(jnp.where((blk < i) & (rank < n_sel), 1.0, 0.0).T)
    qh = [_b(q[:, h * 64:(h + 1) * 64]) for h in range(N_HEADS)]
    causal_bias = jnp.where(_iota((T, T), 1) <= _iota((T, T), 0), 0.0, NEG)

    m_ref[...] = jnp.full(m_ref.shape, NEG, F32)
    for n in range(nb - 1):
        @pl.when(n < i)
        def _():
            _score_block(n, qh, kb_ref, lambda h: jnp.where(sel_cols[h][:, n:n + 1] > 0.5, 0.0, NEG), s_ref, m_ref)
    _score_block(i, qh, kb_ref, lambda h: causal_bias, s_ref, m_ref)
    m_rows = [jnp.max(m_ref[h], axis=-1, keepdims=True) for h in range(N_HEADS)]

    l_ref[...] = jnp.zeros_like(l_ref)
    acc_ref[...] = jnp.zeros_like(acc_ref)
    for n in range(nb - 1):
        @pl.when(n < i)
        def _():
            _value_block(n, m_rows, vb_ref, s_ref, l_ref, acc_ref)
    _value_block(i, m_rows, vb_ref, s_ref, l_ref, acc_ref)
    o_ref[...] = _finish_heads(l_ref, acc_ref) * _silu(g_ref[...])


def _attn_scratch(T, nb):
    return [pltpu.VMEM((N_HEADS, nb, T, MOBA_BLOCK), F32), pltpu.VMEM((N_HEADS, T, LANES), F32),
            pltpu.VMEM((N_HEADS, T, LANES), F32), pltpu.VMEM((T, 256), F32)]


def _moba_prompt(q, kt, vt, g, *, B, L, name):
    T = MOBA_BLOCK
    nb = L // T
    nbp = -(-nb // 8) * 8
    tok = pl.BlockSpec((T, 256), lambda b, i: (b * nb + i, 0))
    seq = pl.BlockSpec((None, 256, L), lambda b, i: (b, 0, 0))
    return pl.pallas_call(
        functools.partial(_moba_prompt_body, L=L, nb=nb),
        grid=(B, nb),
        in_specs=[tok, seq, seq, tok],
        out_specs=tok,
        out_shape=jax.ShapeDtypeStruct((B * L, 256), F32),
        scratch_shapes=[pltpu.VMEM((nb, 256, T), BF16), pltpu.VMEM((nb, 256, T), BF16), pltpu.VMEM((nbp, 256), F32)]
        + _attn_scratch(T, nb),
        compiler_params=_params(2),
        name=name,
    )(q, kt, vt, g)


def _sortable_keys(score):
    bits = pltpu.bitcast(score, jnp.int32)
    keys = jnp.where(bits < 0, bits ^ jnp.int32(0x7FFFFFFF), bits)
    return jnp.where(score == 0.0, 0, keys)


def _sortable_neg_inf():
    return int(np.array(-np.inf, np.float32).view(np.int32)) ^ 0x7FFFFFFF


def _kth_largest(k, count_ge, zero):
    def body(i, ans):
        cand = ans | jnp.left_shift(jnp.int32(1), 31 - i)
        return jnp.where(count_ge(cand ^ jnp.int32(_INT_MIN)) >= k, cand, ans)

    return lax.fori_loop(0, 32, body, zero) ^ jnp.int32(_INT_MIN)


def _fold_rows(x, op):
    while x.shape[0] > SUBLANES:
        half = x.shape[0] // 2
        x = op(x[:half], x[half:])
    return x


def _dsa_prompt_body_t(q_ref, kt_ref, vt_ref, g_ref, qi_ref, wi_ref, kit_ref, o_ref,
                       krm_ref, vb_ref, kirm_ref, keys_ref, bias_ref, thr_ref, cge_ref, cgt_ref,
                       s_ref, m_ref, l_ref, acc_ref, *, L, nb, n_sel):
    i = pl.program_id(1)
    T = MOBA_BLOCK

    @pl.when(i == 0)
    def _():
        for n in range(nb):
            blk = slice(n * T, (n + 1) * T)
            vb_ref[n] = _b(vt_ref[:, blk])
            kirm_ref[n] = _b(kit_ref[:, blk].T)
            for h in range(N_HEADS):
                krm_ref[n, h] = _b(kt_ref[h * 64:(h + 1) * 64, blk].T)

    qi = qi_ref[...]
    qib = [_b(qi[:, h * 64:(h + 1) * 64]) for h in range(DSA_IDX_HEADS)]
    wit = wi_ref[...].T

    def index_block(n):
        kirm = kirm_ref[n]
        sc = wit[0:1, :] * jnp.maximum(_nt(kirm, qib[0]), 0.0)
        for h in range(1, DSA_IDX_HEADS):
            sc = sc + wit[h:h + 1, :] * jnp.maximum(_nt(kirm, qib[h]), 0.0)
        return sc

    for n in range(nb - 1):
        @pl.when(n < i)
        def _():
            keys_ref[n] = _sortable_keys(index_block(n))
    visible = _iota((T, T), 0) <= _iota((T, T), 1)
    keys_ref[i] = _sortable_keys(jnp.where(visible, index_block(i), -jnp.inf))
    for n in range(1, nb):
        @pl.when(n > i)
        def _():
            keys_ref[n] = jnp.full((T, T), _INT_MIN, jnp.int32)

    def threshold_over(n_blocks):
        def count(pred):
            part = jnp.zeros((SUBLANES, T), jnp.int32)
            for n in range(n_blocks):
                part = part + _fold_rows(jnp.where(pred(keys_ref[n]), 1, 0), jnp.add)
            return jnp.sum(part, axis=0, keepdims=True)

        t = _kth_largest(n_sel, lambda c: count(lambda kk: kk >= c), jnp.zeros((1, T), jnp.int32))
        thr_ref[...] = jnp.broadcast_to(t, thr_ref.shape)
        cge_ref[...] = jnp.broadcast_to(count(lambda kk: kk >= t), cge_ref.shape)
        cgt_ref[...] = jnp.broadcast_to(count(lambda kk: kk > t), cgt_ref.shape)

    n_half = max(1, nb // 2)
    if n_half < nb:
        @pl.when(i < n_half)
        def _():
            threshold_over(n_half)

        @pl.when(i >= n_half)
        def _():
            threshold_over(nb)
    else:
        threshold_over(nb)
    thr = thr_ref[0:1, :]
    cnt_ge = cge_ref[0:1, :]
    cnt_gt = cgt_ref[0:1, :]
    few = (i * T + _iota((1, T), 1) + 1) <= n_sel
    neg_inf_key = jnp.int32(_sortable_neg_inf())
    thr_eff = jnp.where(few, neg_inf_key + 1, thr)

    def write_bias(n, carry):
        bias_ref[n] = jnp.where(keys_ref[n] >= thr_eff, 0.0, NEG)
        return carry

    lax.fori_loop(0, i + 1, write_bias, 0)
    tied = jnp.where(few, 0.0, jnp.where(cnt_ge > n_sel, 1.0, 0.0))

    @pl.when(jnp.max(tied) > 0.5)
    def _():
        need = (n_sel - cnt_gt).astype(F32)
        upto_here = jnp.where(_iota((T, T), 1) <= _iota((T, T), 0), 1.0, 0.0).astype(BF16)

        def tie_block(n, seen):
            kc = keys_ref[n]
            eq = jnp.where(kc == thr, 1.0, 0.0)
            upto = _nn(upto_here, _b(eq)) + seen
            pick = jnp.where(kc > thr, 1.0, jnp.where(upto <= need, eq, 0.0))
            tie_bias = jnp.where(pick > 0.5, 0.0, NEG)
            bias_ref[n] = jnp.where(few, jnp.where(kc > neg_inf_key, 0.0, NEG), tie_bias)
            return seen + jnp.sum(eq, axis=0, keepdims=True)

        lax.fori_loop(0, i + 1, tie_block, jnp.zeros((1, T), F32))

    q = q_ref[...]
    qh = [_b(q[:, h * 64:(h + 1) * 64]) for h in range(N_HEADS)]
    m_ref[...] = jnp.full(m_ref.shape, NEG, F32)

    def pass1(n, carry):
        for h in range(N_HEADS):
            st = _nt(krm_ref[n, h], qh[h]) + bias_ref[n]
            s_ref[h, n] = st
            m_ref[h] = jnp.maximum(m_ref[h], _fold_rows(st, jnp.maximum))
        return carry

    lax.fori_loop(0, i + 1, pass1, 0)
    m_rows = [jnp.max(m_ref[h], axis=0, keepdims=True) for h in range(N_HEADS)]
    l_ref[...] = jnp.zeros_like(l_ref)
    acc_ref[...] = jnp.zeros_like(acc_ref)

    def pass2(n, carry):
        vb = vb_ref[n]
        for h in range(N_HEADS):
            hs = slice(h * 64, (h + 1) * 64)
            p = jnp.exp(s_ref[h, n] - m_rows[h])
            l_ref[h] = l_ref[h] + _fold_rows(p, jnp.add)
            acc_ref[hs, :] = acc_ref[hs, :] + _nn(vb[hs, :], _b(p))
        return carry

    lax.fori_loop(0, i + 1, pass2, 0)
    rows = []
    for h in range(N_HEADS):
        rows.append(acc_ref[h * 64:(h + 1) * 64, :] * (1.0 / jnp.sum(l_ref[h], axis=0, keepdims=True)))
    o_ref[...] = jnp.concatenate(rows, axis=0).T * _silu(g_ref[...])


def _dsa_prompt(q, kt, vt, g, qi, wi, kit, *, B, L, name):
    T = MOBA_BLOCK
    nb = L // T
    n_sel = max(1, min(DSA_TOPK, L // 4))
    tok = lambda w: pl.BlockSpec((T, w), lambda b, i: (b * nb + i, 0))
    seq = lambda w: pl.BlockSpec((None, w, L), lambda b, i: (b, 0, 0))
    return pl.pallas_call(
        functools.partial(_dsa_prompt_body_t, L=L, nb=nb, n_sel=n_sel),
        grid=(B, nb),
        in_specs=[tok(256), seq(256), seq(256), tok(256), tok(512), tok(128), seq(64)],
        out_specs=tok(256),
        out_shape=jax.ShapeDtypeStruct((B * L, 256), F32),
        scratch_shapes=[pltpu.VMEM((nb, N_HEADS, T, HEAD_DIM), BF16), pltpu.VMEM((nb, 256, T), BF16),
                        pltpu.VMEM((nb, T, HEAD_DIM), BF16),
                        pltpu.VMEM((nb, T, T), jnp.int32), pltpu.VMEM((nb, T, T), F32),
                        pltpu.VMEM((SUBLANES, T), jnp.int32), pltpu.VMEM((SUBLANES, T), jnp.int32),
                        pltpu.VMEM((SUBLANES, T), jnp.int32),
                        pltpu.VMEM((N_HEADS, nb, T, T), F32), pltpu.VMEM((N_HEADS, SUBLANES, T), F32),
                        pltpu.VMEM((N_HEADS, SUBLANES, T), F32), pltpu.VMEM((256, T), F32)],
        compiler_params=_params(2),
        name=name,
    )(q, kt, vt, g, qi, wi, kit)


def _stack_heads(q):
    return jnp.concatenate([jnp.where(_head_cols(h), q, 0.0) for h in range(N_HEADS)], axis=0)


def _unstack_heads(o):
    out = jnp.zeros((SAMPLE_PAD, 256), F32)
    for h in range(N_HEADS):
        out = out + jnp.where(_head_cols(h), o[h * SAMPLE_PAD:(h + 1) * SAMPLE_PAD, :], 0.0)
    return out


def _page_scores(qs, pages):
    return jnp.concatenate([_nn(qs, _b(p[...])) for p in pages], axis=1)


def _page_values(p, pages, acc):
    for r, page in enumerate(pages):
        acc = acc + _nt(_b(p[:, r * LANES:(r + 1) * LANES]), _b(page[...]))
    return acc


def _own_scores(qs, kn_ref):
    R = qs.shape[0]
    s = _nt(qs, _b(kn_ref[...]))
    qrow = _iota((R, SAMPLE_PAD), 0) & (SAMPLE_PAD - 1)
    return jnp.where(_iota((R, SAMPLE_PAD), 1) <= qrow, s, NEG)


def _softmax_outputs(s_own, s_ref, p_ref, l_ref, pown_ref, n_ch):
    R = s_own.shape[0]
    m_all = jnp.max(s_own, axis=-1, keepdims=True)
    for c in range(n_ch):
        m_all = jnp.maximum(m_all, jnp.max(s_ref[c], axis=-1, keepdims=True))
    p_own = jnp.exp(s_own - m_all)
    l = jnp.sum(p_own, axis=-1, keepdims=True)
    for c in range(n_ch):
        p = jnp.exp(s_ref[c] - m_all)
        p_ref[c] = p
        l = l + jnp.sum(p, axis=-1, keepdims=True)
    l_ref[...] = jnp.broadcast_to(l, l_ref.shape)
    pown_ref[...] = jnp.concatenate([p_own, jnp.zeros((R, LANES - SAMPLE_PAD), F32)], axis=1)


def _moba_sample_body(pt_ref, *refs, pps, n_ch, nblk):
    kp = refs[0:pps]
    q_ref, kn_ref, p_ref, l_ref, pown_ref, qs_ref, gate_ref, s_ref = refs[pps:]
    j = pl.program_id(1)
    R = N_HEADS * SAMPLE_PAD
    W = pps * LANES
    lane = _iota((R, LANES), 1)

    @pl.when(j == 0)
    def _():
        qs_ref[...] = _b(_stack_heads(q_ref[...]))
        gate_ref[...] = jnp.zeros_like(gate_ref)

    s = _page_scores(qs_ref[...], kp)
    s_ref[j] = s
    for r in range(W // MOBA_BLOCK):
        n = j * (W // MOBA_BLOCK) + r
        g = jnp.sum(s[:, r * MOBA_BLOCK:(r + 1) * MOBA_BLOCK], axis=-1, keepdims=True) * (1.0 / MOBA_BLOCK)
        gate_ref[...] = jnp.where((lane & (nblk - 1)) == n, g, gate_ref[...])

    @pl.when(j == n_ch - 1)
    def _():
        gate = gate_ref[...]
        rank = jnp.zeros((R, LANES), jnp.int32)
        for d in range(1, nblk):
            other = pltpu.roll(gate, LANES - d, 1)
            wraps = (lane & (nblk - 1)) + d >= nblk
            rank = rank + jnp.where((other > gate) | ((other == gate) & wraps), 1, 0)
        sel = jnp.where((rank < min(MOBA_TOPK, nblk + 1)) & (lane < nblk), 1.0, 0.0).astype(BF16)
        for c in range(n_ch):
            expand = jnp.where(((c * W + _iota((LANES, W), 1)) >> 8) == _iota((LANES, W), 0), 1.0, 0.0).astype(BF16)
            s_ref[c] = jnp.where(_nn(sel, expand) > 0.5, s_ref[c], NEG)
        _softmax_outputs(_own_scores(qs_ref[...], kn_ref), s_ref, p_ref, l_ref, pown_ref, n_ch)


def _sample_values_body(pt_ref, *refs, pps, n_ch):
    vp = refs[0:pps]
    p_ref, l_ref, pown_ref, vn_ref, g_ref, o_ref, acc_ref = refs[pps:]
    j = pl.program_id(1)

    @pl.when(j == 0)
    def _():
        acc_ref[...] = _nn(_b(pown_ref[:, 0:SAMPLE_PAD]), _b(vn_ref[...]))

    acc_ref[...] = _page_values(p_ref[...], vp, acc_ref[...])

    @pl.when(j == n_ch - 1)
    def _():
        o = acc_ref[...] * (1.0 / l_ref[:, 0:1])
        o_ref[...] = _unstack_heads(o) * _silu(g_ref[...])


def _page_specs(layer, pps, rows):
    def spec(r):
        return pl.BlockSpec((None, None, rows, LANES), lambda b, j, pt: (layer, pt[b, j * pps + r], 0, 0))
    return [spec(r) for r in range(pps)]


def _sample_chunks(page_table):
    n_pages = page_table.shape[1]
    pps = min(SAMPLE_PAGES_PER_STEP, n_pages)
    n_ch = n_pages // pps
    assert n_ch * pps == n_pages and pps % 2 == 0
    return pps, n_ch


def _prob_shapes(B, n_ch, W):
    R = N_HEADS * SAMPLE_PAD
    shapes = [jax.ShapeDtypeStruct((B, n_ch, R, W), F32), jax.ShapeDtypeStruct((B, R, LANES), F32),
              jax.ShapeDtypeStruct((B, R, LANES), F32)]
    specs = [pl.BlockSpec((None, n_ch, R, W), lambda b, j, pt: (b, 0, 0, 0)),
             pl.BlockSpec((None, R, LANES), lambda b, j, pt: (b, 0, 0)),
             pl.BlockSpec((None, R, LANES), lambda b, j, pt: (b, 0, 0))]
    return shapes, specs


def _sample_values(page_table, vt_pool, layer, p, l, pown, vn, g, *, B, name):
    pps, n_ch = _sample_chunks(page_table)
    R = N_HEADS * SAMPLE_PAD
    W = pps * LANES
    tok = pl.BlockSpec((SAMPLE_PAD, 256), lambda b, j, pt: (b, 0))
    row = pl.BlockSpec((None, R, LANES), lambda b, j, pt: (b, 0, 0))
    grid_spec = pltpu.PrefetchScalarGridSpec(
        num_scalar_prefetch=1,
        grid=(B, n_ch),
        in_specs=_page_specs(layer, pps, 256)
        + [pl.BlockSpec((None, None, R, W), lambda b, j, pt: (b, j, 0, 0)), row, row, tok, tok],
        out_specs=tok,
        scratch_shapes=[pltpu.VMEM((R, 256), F32)],
    )
    return pl.pallas_call(
        functools.partial(_sample_values_body, pps=pps, n_ch=n_ch),
        grid_spec=grid_spec,
        out_shape=jax.ShapeDtypeStruct((B * SAMPLE_PAD, 256), F32),
        compiler_params=_params(2),
        name=name,
    )(page_table, *([vt_pool] * pps), p, l, pown, vn, g)


def _moba_sample(page_table, kt_pool, vt_pool, layer, q, kn, vn, g, *, B, name):
    n_pages = page_table.shape[1]
    nblk = n_pages // 2
    assert nblk * 2 == n_pages and LANES % nblk == 0
    pps, n_ch = _sample_chunks(page_table)
    R = N_HEADS * SAMPLE_PAD
    W = pps * LANES
    tok = pl.BlockSpec((SAMPLE_PAD, 256), lambda b, j, pt: (b, 0))
    out_shapes, out_specs = _prob_shapes(B, n_ch, W)
    grid_spec = pltpu.PrefetchScalarGridSpec(
        num_scalar_prefetch=1,
        grid=(B, n_ch),
        in_specs=_page_specs(layer, pps, 256) + [tok, tok],
        out_specs=out_specs,
        scratch_shapes=[pltpu.VMEM((R, 256), BF16), pltpu.VMEM((R, LANES), F32), pltpu.VMEM((n_ch, R, W), F32)],
    )
    p, l, pown = pl.pallas_call(
        functools.partial(_moba_sample_body, pps=pps, n_ch=n_ch, nblk=nblk),
        grid_spec=grid_spec,
        out_shape=out_shapes,
        compiler_params=_params(2),
        name=name + "_scores",
    )(page_table, *([kt_pool] * pps), q, kn)
    return _sample_values(page_table, vt_pool, layer, p, l, pown, vn, g, B=B, name=name + "_values")


def _dsa_sample_body(pt_ref, *refs, pps, n_ch, n_sel, past):
    ikp = refs[0:pps]
    kp = refs[pps:2 * pps]
    (q_ref, kn_ref, qi_ref, wi_ref, kin_ref, p_ref, l_ref, pown_ref,
     qs_ref, qis_ref, wis_ref, sc_ref, scn_ref, bias_ref, biasn_ref, s_ref) = refs[2 * pps:]
    j = pl.program_id(1)
    R = N_HEADS * SAMPLE_PAD
    RI = DSA_IDX_HEADS * SAMPLE_PAD
    W = pps * LANES

    def index_scores(a):
        a = jnp.maximum(a, 0.0) * wis_ref[:, 0:1]
        out = a[0:SAMPLE_PAD, :]
        for h in range(1, DSA_IDX_HEADS):
            out = out + a[h * SAMPLE_PAD:(h + 1) * SAMPLE_PAD, :]
        return out

    @pl.when(j == 0)
    def _():
        qs_ref[...] = _b(_stack_heads(q_ref[...]))
        qi = qi_ref[...]
        wi = wi_ref[...]
        qis_ref[...] = _b(jnp.concatenate([qi[:, h * 64:(h + 1) * 64] for h in range(DSA_IDX_HEADS)], axis=0))
        wcol = jnp.concatenate([wi[:, h:h + 1] for h in range(DSA_IDX_HEADS)], axis=0)
        wis_ref[...] = jnp.broadcast_to(wcol, (RI, LANES))

    sc_ref[j] = index_scores(_page_scores(qis_ref[...], ikp))
    s_ref[j] = _page_scores(qs_ref[...], kp)

    @pl.when(j == n_ch - 1)
    def _():
        s_new = index_scores(_nt(qis_ref[...], _b(kin_ref[...])))
        ok = _iota((SAMPLE_PAD, SAMPLE_PAD), 1) <= _iota((SAMPLE_PAD, SAMPLE_PAD), 0)
        s_new = jnp.where(ok, s_new, -jnp.inf)
        scn_ref[...] = jnp.concatenate([s_new, jnp.full((SAMPLE_PAD, LANES - SAMPLE_PAD), -jnp.inf, F32)], axis=1)
        keys = _sortable_keys(sc_ref[...])
        keys_new = _sortable_keys(scn_ref[...])

        def count(pred):
            past_cnt = jnp.sum(jnp.sum(jnp.where(pred(keys), 1, 0), axis=0), axis=-1, keepdims=True)
            return past_cnt + jnp.sum(jnp.where(pred(keys_new), 1, 0), axis=-1, keepdims=True)

        thr = _kth_largest(n_sel, lambda t: count(lambda kk: kk >= t), jnp.zeros((SAMPLE_PAD, 1), jnp.int32))
        cnt_ge = count(lambda kk: kk >= thr)
        cnt_gt = count(lambda kk: kk > thr)
        few = (past + _iota((SAMPLE_PAD, 1), 0) + 1) <= n_sel
        neg_inf_key = jnp.int32(_sortable_neg_inf())
        thr_eff = jnp.where(few, neg_inf_key + 1, thr)
        bias_ref[...] = jnp.where(keys >= thr_eff, 0.0, NEG)
        biasn_ref[...] = jnp.where(keys_new >= thr_eff, 0.0, NEG)
        tied = jnp.where(few, 0.0, jnp.where(cnt_ge > n_sel, 1.0, 0.0))

        @pl.when(jnp.max(tied) > 0.5)
        def _():
            need = (n_sel - cnt_gt).astype(F32)
            tri = jnp.where(_iota((LANES, LANES), 0) <= _iota((LANES, LANES), 1), 1.0, 0.0).astype(BF16)

            def tie_lanes(kc, seen):
                eq = jnp.where(kc == thr, 1.0, 0.0)
                upto = _nn(_b(eq), tri) + seen
                pick = jnp.where(kc > thr, 1.0, jnp.where(upto <= need, eq, 0.0))
                tie_bias = jnp.where(pick > 0.5, 0.0, NEG)
                bias = jnp.where(few, jnp.where(kc > neg_inf_key, 0.0, NEG), tie_bias)
                return bias, seen + jnp.sum(eq, axis=-1, keepdims=True)

            def tie_chunk(c, seen):
                kch = _sortable_keys(sc_ref[c])
                for r in range(pps):
                    cs = slice(r * LANES, (r + 1) * LANES)
                    bias, seen = tie_lanes(kch[:, cs], seen)
                    bias_ref[c, :, cs] = bias
                return seen

            seen = lax.fori_loop(0, n_ch, tie_chunk, jnp.zeros((SAMPLE_PAD, 1), F32))
            bias, _ = tie_lanes(keys_new, seen)
            biasn_ref[...] = bias

        for c in range(n_ch):
            s_ref[c] = s_ref[c] + jnp.concatenate([bias_ref[c]] * N_HEADS, axis=0)
        s_own = _own_scores(qs_ref[...], kn_ref) + jnp.concatenate([biasn_ref[:, 0:SAMPLE_PAD]] * N_HEADS, axis=0)
        _softmax_outputs(s_own, s_ref, p_ref, l_ref, pown_ref, n_ch)


def _dsa_sample(page_table, kt_pool, vt_pool, ikt_pool, layer, q, kn, vn, g, qi, wi, kin, *, B, valid, name):
    n_pages = page_table.shape[1]
    past = n_pages * LANES
    pps, n_ch = _sample_chunks(page_table)
    n_sel = max(1, min(DSA_TOPK, (past + valid) // 4))
    R = N_HEADS * SAMPLE_PAD
    RI = DSA_IDX_HEADS * SAMPLE_PAD
    W = pps * LANES
    tok = lambda w: pl.BlockSpec((SAMPLE_PAD, w), lambda b, j, pt: (b, 0))
    out_shapes, out_specs = _prob_shapes(B, n_ch, W)
    grid_spec = pltpu.PrefetchScalarGridSpec(
        num_scalar_prefetch=1,
        grid=(B, n_ch),
        in_specs=_page_specs(layer, pps, 64) + _page_specs(layer, pps, 256)
        + [tok(256), tok(256), tok(512), tok(128), tok(64)],
        out_specs=out_specs,
        scratch_shapes=[pltpu.VMEM((R, 256), BF16), pltpu.VMEM((RI, 64), BF16), pltpu.VMEM((RI, LANES), F32),
                        pltpu.VMEM((n_ch, SAMPLE_PAD, W), F32), pltpu.VMEM((SAMPLE_PAD, LANES), F32),
                        pltpu.VMEM((n_ch, SAMPLE_PAD, W), F32), pltpu.VMEM((SAMPLE_PAD, LANES), F32),
                        pltpu.VMEM((n_ch, R, W), F32)],
    )
    p, l, pown = pl.pallas_call(
        functools.partial(_dsa_sample_body, pps=pps, n_ch=n_ch, n_sel=n_sel, past=past),
        grid_spec=grid_spec,
        out_shape=out_shapes,
        compiler_params=_params(2),
        name=name + "_scores",
    )(page_table, *([ikt_pool] * pps), *([kt_pool] * pps), q, kn, qi, wi, kin)
    return _sample_values(page_table, vt_pool, layer, p, l, pown, vn, g, B=B, name=name + "_values")


def _out_proj_body(a_ref, b_ref, c_ref, d_ref, x_ref, wo_ref, g1_ref, g2_ref, wq_ref, x1_ref, q_ref):
    acc = _nn(_b(a_ref[...]), wo_ref[0:256, :])
    acc = acc + _nn(_b(b_ref[...]), wo_ref[256:512, :])
    acc = acc + _nn(_b(c_ref[...]), wo_ref[512:768, :])
    acc = acc + _nn(_b(d_ref[...]), wo_ref[768:1024, :])
    x1 = x_ref[...] + _rms(acc, g1_ref[...])
    x1_ref[...] = x1
    q_ref[...] = _nn(_b(_rms(x1, g2_ref[...])), wq_ref[...]) * (XA_DIM ** -0.5)


def _out_proj(parts, x, wo, g1, g2, wq, *, name):
    n, d = x.shape
    tm = min(256, n)
    tok = lambda w: pl.BlockSpec((tm, w), lambda i: (i, 0))
    const = lambda a: pl.BlockSpec(a.shape, lambda i: (0,) * a.ndim)
    return pl.pallas_call(
        _out_proj_body,
        grid=(n // tm,),
        in_specs=[tok(256)] * 4 + [tok(d), const(wo), const(g1), const(g2), const(wq)],
        out_specs=[tok(d), tok(d)],
        out_shape=[jax.ShapeDtypeStruct((n, d), F32)] * 2,
        compiler_params=_params(1),
        name=name,
    )(*parts, x, wo, g1, g2, wq)


def _xattn_body(q_ref, mk_ref, mv_ref, wo_ref, g_ref, x_ref, o_ref):
    head_major = len(mk_ref.shape) == 3
    acc = jnp.zeros(x_ref.shape, F32)
    for h in range(N_HEADS):
        hs = slice(h * XA_DIM, (h + 1) * XA_DIM)
        mk = mk_ref[:, h, :] if head_major else mk_ref[:, hs]
        mv = mv_ref[:, h, :] if head_major else mv_ref[:, hs]
        s = _nt(_b(q_ref[:, hs]), _b(mk))
        m = jnp.max(s, axis=-1, keepdims=True)
        p = jnp.exp(s - m)
        oh = _nn(_b(p), _b(mv)) * (1.0 / jnp.sum(p, axis=-1, keepdims=True))
        acc = acc + _nn(_b(oh), wo_ref[hs, :])
    o_ref[...] = x_ref[...] + _rms(acc, g_ref[...])


def _xattn(q, mem_k, mem_v, layer, wo, g, x, *, B, L, name):
    T = min(256, L)
    nt = L // T
    d = x.shape[1]
    m = mem_k.shape[2]
    tok = pl.BlockSpec((T, d), lambda b, t: (b * nt + t, 0))
    if mem_k.ndim == 5:
        mem = pl.BlockSpec((None, None, m, N_HEADS, XA_DIM), lambda b, t: (layer, b, 0, 0, 0))
    else:
        mem = pl.BlockSpec((None, None, m, d), lambda b, t: (layer, b, 0, 0))
    const = lambda a: pl.BlockSpec(a.shape, lambda b, t: (0,) * a.ndim)
    return pl.pallas_call(
        _xattn_body,
        grid=(B, nt),
        in_specs=[tok, mem, mem, const(wo), const(g), tok],
        out_specs=tok,
        out_shape=jax.ShapeDtypeStruct(x.shape, F32),
        compiler_params=_params(2),
        name=name,
    )(q, mem_k, mem_v, wo, g, x)


def _pad_cols(w, width):
    return jnp.pad(w, ((0, 0), (0, width - w.shape[1])))


_IN_SEGS = (
    ("g_q", 0, 128, False, GLA_DK ** -0.5, 128), ("g_k", 1, 128, False, 1.0, 128), ("g_v", 2, 256, False, 1.0, 256),
    ("g_gate", 3, 256, False, 1.0, 256), ("g_lr", 4, 128, False, 1.0, 128),
    ("s_z", 5, 256, False, 1.0, 256), ("s_xbc", 6, 768, False, 1.0, 768), ("s_dt", 7, 128, False, 1.0, 128),
    ("m_q", 8, 256, True, HEAD_DIM ** -0.5, 256), ("m_k", 9, 256, True, 1.0, 256), ("m_v", 10, 256, False, 1.0, 256),
    ("m_gate", 11, 256, False, 1.0, 256),
    ("d_q", 12, 256, True, HEAD_DIM ** -0.5, 256), ("d_k", 13, 256, True, 1.0, 256), ("d_v", 14, 256, False, 1.0, 256),
    ("d_gate", 15, 256, False, 1.0, 256), ("d_qi", 16, 512, True, HEAD_DIM ** -0.5, 512),
    ("d_ki", 17, 128, True, 1.0, 64), ("d_wi", 18, 128, False, DSA_IDX_HEADS ** -0.5, 128),
)
_KV_SEGS = (("m_k", 9, 256, True), ("m_v", 10, 256, False), ("d_k", 13, 256, True), ("d_v", 14, 256, False),
            ("d_ki", 17, 64, True))


def _in_proj_layout(transposed_kv):
    offs = [int(o) for o in np.concatenate([[0], np.cumsum(IN_SPLITS)])]
    kv_names = {s[0] for s in _KV_SEGS} if transposed_kv else set()
    segs, names = [], []
    for name, idx, _, rope, scale, ow in _IN_SEGS:
        if name not in kv_names:
            segs.append((offs[idx], IN_SPLITS[idx], rope, scale, ow))
            names.append(name)
    tsegs, tnames = [], []
    if transposed_kv:
        for name, idx, width, rope in _KV_SEGS:
            tsegs.append((offs[idx], width, rope))
            tnames.append(name)
    return tuple(segs), names, tuple(tsegs), tnames


def _rope_tables(pos, rows):
    half = HEAD_DIM // 2
    inv = ROPE_THETA ** (-jnp.arange(half, dtype=F32) / half)
    ang = pos.astype(F32)[:, None] * inv[None, :]
    cos, sin = jnp.cos(ang), jnp.sin(ang)
    cs1 = jnp.concatenate([cos, cos], axis=1)
    sn1 = jnp.concatenate([-sin, sin], axis=1)
    cs = jnp.tile(cs1, (1, 512 // HEAD_DIM))
    sn = jnp.tile(sn1, (1, 512 // HEAD_DIM))
    if cs.shape[0] < rows:
        cs = jnp.tile(cs, (rows // cs.shape[0], 1))
        sn = jnp.tile(sn, (rows // sn.shape[0], 1))
    cst = jnp.tile(cs1.T, (256 // HEAD_DIM, 1))
    snt = jnp.tile(sn1.T, (256 // HEAD_DIM, 1))
    return (cs, sn), (cst, snt)


def _head_rep(v4, widths):
    return jnp.concatenate([jnp.repeat(v4, w) for w in widths]).reshape(1, -1)


def _layer_consts(l, W):
    e = np.zeros((128, 768), np.float32)
    for h in range(N_HEADS):
        e[h, h * 64:(h + 1) * 64] = 1.0
        e[h, 256 + h * 128:256 + (h + 1) * 128] = 1.0
    ssd_consts = (jnp.asarray(e),
                  jnp.pad(W["ssd_conv_w"][l], ((0, 8 - SSD_CONV), (0, 0))),
                  W["ssd_conv_b"][l].reshape(1, -1),
                  _head_rep(W["ssd_dt_bias"][l], (64, 128)),
                  _head_rep(W["ssd_a_log"][l], (64, 128)),
                  _head_rep(W["ssd_d"][l], (64,)),
                  W["ssd_norm"][l].reshape(1, -1))
    return dict(
        w_in=_b(_pad_cols(W["w_in"][l], -(-(sum(IN_SPLITS) + LANES) // LANES) * LANES)),
        wlr=jnp.pad(W["gla_w_lr"][l], ((0, 128 - GLA_RANK), (0, 0))),
        blr=W["gla_b_lr"][l].reshape(1, -1),
        gn=jnp.tile(W["gla_norm"][l], N_HEADS).reshape(1, -1),
        ssd=ssd_consts,
        wo=_b(W["w_out"][l]), wq=_b(W["xa_wq"][l]), xwo=_b(W["xa_wo"][l]),
        wkv=_b(jnp.concatenate([W["xa_wk"][l], W["xa_wv"][l]], axis=1)),
    )


def _gla_state_to_t(s):
    B = s.shape[0]
    out = jnp.zeros((B, N_HEADS, 64, N_HEADS, GLA_DK), F32)
    for h in range(N_HEADS):
        out = out.at[:, h, :, h, :].set(jnp.swapaxes(s[:, h], 1, 2))
    return out.reshape(B, 256, 128)


def _gla_state_from_t(st):
    B = st.shape[0]
    s5 = st.reshape(B, N_HEADS, 64, N_HEADS, GLA_DK)
    return jnp.stack([jnp.swapaxes(s5[:, h, :, h, :], 1, 2) for h in range(N_HEADS)], axis=1)


def _mixer_layer(x, l, C, W, tabs, *, B, L, valid, gla_s0t, ssd_s0, ssd_s0_idx, conv0, mem_k, mem_v, mem_idx,
                 sample_ctx, tag):
    rope, rope_t = tabs
    segs, names, tsegs, tnames = _in_proj_layout(sample_ctx is None)
    outs = _norm_proj(x, W["norm_mix_pre"][l], C["w_in"], segs, rope, tsegs, rope_t, L, name=f"in_proj_{tag}")
    P = dict(zip(names + tnames, outs))
    o_gla, gla_st = _gla(P["g_q"], P["g_k"], P["g_v"], P["g_gate"], P["g_lr"], C["wlr"], C["blr"], C["gn"], gla_s0t,
                         B=B, L=L, valid=valid, name=f"gla_{tag}")
    o_ssd, ssd_s, conv_new = _ssd(P["s_z"], P["s_xbc"], P["s_dt"], conv0, C["ssd"], ssd_s0, ssd_s0_idx,
                                  B=B, L=L, valid=valid, name=f"ssd_{tag}")
    if sample_ctx is None:
        o_moba = _moba_prompt(P["m_q"], P["m_k"], P["m_v"], P["m_gate"], B=B, L=L, name=f"moba_{tag}")
        o_dsa = _dsa_prompt(P["d_q"], P["d_k"], P["d_v"], P["d_gate"], P["d_qi"], P["d_wi"], P["d_ki"],
                            B=B, L=L, name=f"dsa_{tag}")
    else:
        pt, mk_pool, mv_pool, dk_pool, dv_pool, ik_pool = sample_ctx
        o_moba = _moba_sample(pt, mk_pool, mv_pool, l, P["m_q"], P["m_k"], P["m_v"], P["m_gate"], B=B,
                              name=f"moba_{tag}")
        o_dsa = _dsa_sample(pt, dk_pool, dv_pool, ik_pool, l, P["d_q"], P["d_k"], P["d_v"], P["d_gate"],
                            P["d_qi"], P["d_wi"], P["d_ki"], B=B, valid=valid, name=f"dsa_{tag}")
    x1, xq = _out_proj((o_gla, o_ssd, o_moba, o_dsa), x, C["wo"], W["norm_mix_post"][l].reshape(1, -1),
                       W["norm_xa_pre"][l].reshape(1, -1), C["wq"], name=f"out_proj_{tag}")
    x2 = _xattn(xq, mem_k, mem_v, mem_idx, C["xwo"], W["norm_xa_post"][l].reshape(1, -1), x1, B=B, L=L,
                name=f"xattn_{tag}")
    new = (gla_st, ssd_s, conv_new, P["m_k"], P["m_v"], P["d_k"], P["d_v"], P["d_ki"])
    return x2, new


def kernel(x_prompt, x_sample, state_gla, state_ssd, state_conv, cache_moba_k, cache_moba_v,
           cache_dsa_k, cache_dsa_v, cache_dsa_ik, cache_mem_k, cache_mem_v, page_table, mem_prompt,
           norm_mix_pre, norm_mix_post, norm_xa_pre, norm_xa_post, norm_mem, w_in, gla_w_lr, gla_b_lr,
           gla_norm, ssd_conv_w, ssd_conv_b, ssd_dt_bias, ssd_a_log, ssd_d, ssd_norm, w_out,
           xa_wq, xa_wk, xa_wv, xa_wo):
    W = dict(norm_mix_pre=norm_mix_pre, norm_mix_post=norm_mix_post, norm_xa_pre=norm_xa_pre,
             norm_xa_post=norm_xa_post, norm_mem=norm_mem, w_in=w_in, gla_w_lr=gla_w_lr, gla_b_lr=gla_b_lr,
             gla_norm=gla_norm, ssd_conv_w=ssd_conv_w, ssd_conv_b=ssd_conv_b, ssd_dt_bias=ssd_dt_bias,
             ssd_a_log=ssd_a_log, ssd_d=ssd_d, ssd_norm=ssd_norm, w_out=w_out, xa_wq=xa_wq, xa_wk=xa_wk,
             xa_wv=xa_wv, xa_wo=xa_wo)
    depth = w_in.shape[0]
    Bp, S, D = x_prompt.shape
    Bs, Ls, _ = x_sample.shape
    M = mem_prompt.shape[1]
    n_pool, page = cache_moba_k.shape[1], cache_moba_k.shape[2]
    assert page == LANES and Ls <= SAMPLE_PAD and S % MOBA_BLOCK == 0
    past = page_table.shape[1] * page
    consts = [_layer_consts(l, W) for l in range(depth)]

    tabs_p = _rope_tables(jnp.arange(S, dtype=jnp.int32), min(256, Bp * S))
    zero_gla = jnp.zeros((Bp, 256, 128), F32)
    zero_ssd = jnp.zeros((1, Bp, 256, 128), F32)
    zero_conv = jnp.zeros((Bp, 8, 768), F32)
    mem_flat = mem_prompt.reshape(Bp * M, D)
    kv_segs = ((0, D, False, 1.0, D), (D, D, False, 1.0, D))
    x = x_prompt.reshape(Bp * S, D)
    p_new = []
    for l in range(depth):
        C = consts[l]
        mk, mv = _norm_proj(mem_flat, norm_mem[l], C["wkv"], kv_segs, name="mem_kv")
        x, new = _mixer_layer(x, l, C, W, tabs_p, B=Bp, L=S, valid=S, gla_s0t=zero_gla, ssd_s0=zero_ssd,
                              ssd_s0_idx=0, conv0=zero_conv, mem_k=mk.reshape(1, Bp, M, D),
                              mem_v=mv.reshape(1, Bp, M, D), mem_idx=0, sample_ctx=None, tag="p")
        p_new.append(new + (mk.reshape(Bp, M, N_HEADS, XA_DIM), mv.reshape(Bp, M, N_HEADS, XA_DIM)))
    y_prompt = x.reshape(Bp, S, D)
    (p_gla, p_ssd, p_conv, p_mk, p_mv, p_dk, p_dv, p_ik, p_memk, p_memv) = [jnp.stack(t) for t in zip(*p_new)]
    p_gla = _gla_state_from_t(p_gla.reshape(depth * Bp, 256, 128)).reshape(depth, Bp, N_HEADS, GLA_DK, 64)
    p_ssd = p_ssd.reshape(depth, Bp, N_HEADS, 64, SSD_N)
    p_conv = p_conv[:, :, :SSD_CONV - 1]
    kv5 = lambda a: jnp.transpose(a.reshape(depth, Bp, N_HEADS, HEAD_DIM, S), (0, 1, 4, 2, 3))
    p_ik = jnp.transpose(p_ik, (0, 1, 3, 2))

    Lp = SAMPLE_PAD
    tabs_s = _rope_tables(past + jnp.arange(Lp, dtype=jnp.int32), min(256, Bs * Lp))
    xs = jnp.pad(x_sample, ((0, 0), (0, Lp - Ls), (0, 0))).reshape(Bs * Lp, D)
    gla_s0_all = _gla_state_to_t(state_gla.reshape(depth * Bs, N_HEADS, GLA_DK, 64)).reshape(depth, Bs, 256, 128)
    ssd_s0_all = state_ssd.reshape(depth, Bs, 256, SSD_N)
    conv0_all = jnp.pad(state_conv, ((0, 0), (0, 0), (8 - (SSD_CONV - 1), 0), (0, 0)))
    kv_t = lambda a: jnp.transpose(a, (0, 1, 3, 4, 2)).reshape(depth, n_pool, N_HEADS * HEAD_DIM, page)
    pools = (kv_t(cache_moba_k), kv_t(cache_moba_v), kv_t(cache_dsa_k), kv_t(cache_dsa_v),
             jnp.transpose(cache_dsa_ik, (0, 1, 3, 2)))
    s_new = []
    for l in range(depth):
        xs, new = _mixer_layer(xs, l, consts[l], W, tabs_s, B=Bs, L=Lp, valid=Ls, gla_s0t=gla_s0_all[l],
                               ssd_s0=ssd_s0_all, ssd_s0_idx=l, conv0=conv0_all[l], mem_k=cache_mem_k,
                               mem_v=cache_mem_v, mem_idx=l, sample_ctx=(page_table,) + pools, tag="s")
        s_new.append(new)
    y_sample = xs.reshape(Bs, Lp, D)[:, :Ls]
    (s_gla, s_ssd, s_conv, s_mk, s_mv, s_dk, s_dv, s_ik) = [jnp.stack(t) for t in zip(*s_new)]
    s_gla = _gla_state_from_t(s_gla.reshape(depth * Bs, 256, 128)).reshape(depth, Bs, N_HEADS, GLA_DK, 64)
    s_ssd = s_ssd.reshape(depth, Bs, N_HEADS, 64, SSD_N)
    s_conv = s_conv[:, :, :SSD_CONV - 1]
    skv5 = lambda a: a.reshape(depth, Bs, Lp, N_HEADS, HEAD_DIM)[:, :, :Ls]

    return (y_prompt, y_sample, p_gla, p_ssd, p_conv, kv5(p_mk), kv5(p_mv), kv5(p_dk), kv5(p_dv),
            p_ik, p_memk, p_memv,
            s_gla, s_ssd, s_conv, skv5(s_mk), skv5(s_mv), skv5(s_dk), skv5(s_dv),
            s_ik.reshape(depth, Bs, Lp, HEAD_DIM)[:, :, :Ls])
```

```python
import functools

import jax
import jax.numpy as jnp
import numpy as np
from jax import lax
from jax.experimental import pallas as pl
from jax.experimental.pallas import tpu as pltpu

F32 = jnp.float32
BF16 = jnp.bfloat16
HI = lax.Precision.HIGHEST

D_MODEL = 1024
HEAD_DIM = 64
ROPE_THETA = 10000.0
EPS = 1e-6
N_HEADS = 4
GLA_DK = 32
GLA_RANK = 16
GLA_TAU = 16.0
SCAN_CHUNK = 64
SSD_N = 128
SSD_CONV = 4
MOBA_BLOCK = 256
MOBA_TOPK = 3
DSA_IDX_HEADS = 8
DSA_TOPK = 256
XA_DIM = 256
IN_SPLITS = (128, 128, 256, 256, 16, 256, 768, 4, 256, 256, 256, 256, 256, 256, 256, 256, 512, 64, 8)

LANES = 128
SUBLANES = 8
SAMPLE_PAD = SUBLANES
SAMPLE_PAGES_PER_STEP = 16
VMEM_LIMIT = 56 * 1024 * 1024
NEG = -1e30
_INT_MIN = -2 ** 31


def _nn(a, b, precision=None):
    return jnp.dot(a, b, preferred_element_type=F32, precision=precision)


def _nt(a, b, precision=None):
    return lax.dot_general(a, b, (((1,), (1,)), ((), ())), preferred_element_type=F32, precision=precision)


def _tn(a, b, precision=None):
    return lax.dot_general(a, b, (((0,), (0,)), ((), ())), preferred_element_type=F32, precision=precision)


def _b(x):
    return x.astype(BF16)


def _sigmoid(x):
    return 1.0 / (1.0 + jnp.exp(-x))


def _silu(x):
    return x * _sigmoid(x)


def _softplus(x):
    return jnp.maximum(x, 0.0) + jnp.log1p(jnp.exp(-jnp.abs(x)))


def _log_sigmoid(x):
    return jnp.minimum(x, 0.0) - jnp.log1p(jnp.exp(-jnp.abs(x)))


def _rms(x, g):
    return x * lax.rsqrt(jnp.mean(x * x, axis=-1, keepdims=True) + EPS) * g


def _iota(shape, dim):
    return lax.broadcasted_iota(jnp.int32, shape, dim)


def _fold(x):
    return x[:, 0:LANES], x[:, LANES:2 * LANES]


def _params(n_axes):
    return pltpu.CompilerParams(dimension_semantics=("arbitrary",) * n_axes, vmem_limit_bytes=VMEM_LIMIT)


def _norm_proj_body(*refs, segs, tsegs, has_rope):
    refs = list(refs)
    x_ref, g_ref, w_ref = refs[:3]
    del refs[:3]
    cs_ref, sn_ref = (refs.pop(0), refs.pop(0)) if has_rope else (None, None)
    cst_ref, snt_ref = (refs.pop(0), refs.pop(0)) if (has_rope and tsegs) else (None, None)
    y_ref = refs.pop()
    outs = refs
    y_ref[...] = _nn(_b(_rms(x_ref[...], g_ref[...])), w_ref[...])
    for (c0, w, rope, scale, ow), o_ref in zip(segs, outs):
        if ow > w:
            y = jnp.where(_iota((1, ow), 1) < w, y_ref[:, c0:c0 + ow], 0.0)
        else:
            y = y_ref[:, c0:c0 + w]
        if rope and w % LANES == 0:
            lane = _iota(y.shape, 1)
            sw = jnp.where((lane & 63) < 32, pltpu.roll(y, w - 32, 1), pltpu.roll(y, 32, 1))
            y = y * cs_ref[:, :w] + sw * sn_ref[:, :w]
        elif rope:
            halves = []
            for h in range(w // HEAD_DIM):
                halves += [y[:, h * 64 + 32:h * 64 + 64], y[:, h * 64:h * 64 + 32]]
            y = y * cs_ref[:, :w] + jnp.concatenate(halves, axis=1) * sn_ref[:, :w]
        if scale != 1.0:
            y = y * scale
        o_ref[...] = y
    for (c0, w, rope), o_ref in zip(tsegs, outs[len(segs):]):
        yt = y_ref[:, c0:c0 + w].T
        if rope:
            halves = []
            for h in range(w // HEAD_DIM):
                halves += [yt[h * 64 + 32:h * 64 + 64, :], yt[h * 64:h * 64 + 32, :]]
            yt = yt * cst_ref[0:w, :] + jnp.concatenate(halves, axis=0) * snt_ref[0:w, :]
        o_ref[...] = yt


def _norm_proj(x, gain, w_cat, segs, rope_tabs=None, tsegs=(), rope_tabs_t=None, seq_len=None, *, name):
    n, d = x.shape
    tm = min(256, n)
    has_rope = rope_tabs is not None
    in_specs = [pl.BlockSpec((tm, d), lambda i: (i, 0)),
                pl.BlockSpec((1, d), lambda i: (0, 0)),
                pl.BlockSpec(w_cat.shape, lambda i: (0, 0))]
    args = [x, gain.reshape(1, d), w_cat]
    if has_rope:
        cs, sn = rope_tabs
        nt = cs.shape[0] // tm
        in_specs += [pl.BlockSpec((tm, cs.shape[1]), lambda i: (i % nt, 0))] * 2
        args += [cs, sn]
    out_specs = [pl.BlockSpec((tm, s[4]), lambda i: (i, 0)) for s in segs]
    out_shape = [jax.ShapeDtypeStruct((n, s[4]), F32) for s in segs]
    if tsegs:
        nts = seq_len // tm
        assert nts * tm == seq_len
        if has_rope:
            cst, snt = rope_tabs_t
            in_specs += [pl.BlockSpec((cst.shape[0], tm), lambda i: (0, i % nts))] * 2
            args += [cst, snt]
        out_specs += [pl.BlockSpec((None, s[1], tm), lambda i: (i // nts, 0, i % nts)) for s in tsegs]
        out_shape += [jax.ShapeDtypeStruct((n // seq_len, s[1], seq_len), F32) for s in tsegs]
    return pl.pallas_call(
        functools.partial(_norm_proj_body, segs=segs, tsegs=tuple(tsegs), has_rope=has_rope),
        grid=(n // tm,),
        in_specs=in_specs,
        out_specs=out_specs,
        out_shape=out_shape,
        scratch_shapes=[pltpu.VMEM((tm, w_cat.shape[1]), F32)],
        compiler_params=_params(1),
        name=name,
    )(*args)


def _gla_body(q_ref, k_ref, v_ref, g_ref, lr_ref, wlr_ref, blr_ref, gn_ref, s0_ref, o_ref, so_ref, st_ref,
              *, T, C, valid, nt):
    t = pl.program_id(1)

    @pl.when(t == 0)
    def _():
        st_ref[...] = s0_ref[...]

    causal = _iota((C, C), 1) <= _iota((C, C), 0)
    lane_qk = _iota((1, 128), 1)
    lane_v = _iota((1, 256), 1)
    bd_state = (_iota((256, 128), 0) >> 6) == (_iota((256, 128), 1) >> 5)
    bd_head = ((_iota((256, 256), 0) >> 6) == (_iota((256, 256), 1) >> 6)).astype(F32)
    logf = _log_sigmoid(_nn(lr_ref[...], wlr_ref[...], HI) + blr_ref[...]) * (1.0 / GLA_TAU)
    if valid < T:
        logf = jnp.where(_iota((T, 1), 0) < valid, logf, 0.0)
    trow = _iota((T, T), 0)
    tcol = _iota((T, T), 1)
    shift = C.bit_length() - 1
    assert (1 << shift) == C
    same_chunk_upto = jnp.where((tcol <= trow) & ((trow >> shift) == (tcol >> shift)), 1.0, 0.0)
    gcum_all = _nn(same_chunk_upto, logf, HI)
    st = st_ref[...]
    for c in range(T // C):
        sl = slice(c * C, (c + 1) * C)
        q = q_ref[sl, :]
        k = k_ref[sl, :]
        v = v_ref[sl, :]
        if valid < T:
            k = jnp.where(_iota((C, 1), 0) < valid, k, 0.0)
        gcum = gcum_all[sl, :]
        gend = gcum[C - 1:C, :]
        qd = q * jnp.exp(gcum)
        kinv = k * jnp.exp(-gcum)
        kend = k * jnp.exp(gend - gcum)
        o = _nt(_b(qd), _b(st))
        for h in range(N_HEADS):
            att = _nt(_b(jnp.where((lane_qk >> 5) == h, qd, 0.0)), _b(kinv))
            att = jnp.where(causal, att, 0.0)
            o = o + _nn(_b(att), _b(jnp.where((lane_v >> 6) == h, v, 0.0)))
        st = st * jnp.exp(gend) + jnp.where(bd_state, _tn(_b(v), _b(kend)), 0.0)
        o_ref[sl, :] = o
    st_ref[...] = st
    o = o_ref[...]
    ms = _nn(o * o, bd_head, HI) * (1.0 / HEAD_DIM)
    o_ref[...] = o * lax.rsqrt(ms + EPS) * gn_ref[...] * _silu(g_ref[...])

    @pl.when(t == nt - 1)
    def _():
        so_ref[...] = st


def _gla(q, k, v, g, lr, wlr, blr, gn, s0t, *, B, L, valid, name):
    T = min(256, L)
    C = min(SCAN_CHUNK, T)
    nt = L // T
    assert valid == L or nt == 1
    tok = lambda w: pl.BlockSpec((T, w), lambda b, t: (b * nt + t, 0))
    const = lambda a: pl.BlockSpec(a.shape, lambda b, t: (0,) * a.ndim)
    st_spec = pl.BlockSpec((None, 256, 128), lambda b, t: (b, 0, 0))
    return pl.pallas_call(
        functools.partial(_gla_body, T=T, C=C, valid=valid, nt=nt),
        grid=(B, nt),
        in_specs=[tok(128), tok(128), tok(256), tok(256), tok(128), const(wlr), const(blr), const(gn), st_spec],
        out_specs=[tok(256), st_spec],
        out_shape=[jax.ShapeDtypeStruct((B * L, 256), F32), jax.ShapeDtypeStruct((B, 256, 128), F32)],
        scratch_shapes=[pltpu.VMEM((256, 128), F32)],
        compiler_params=_params(2),
        name=name,
    )(q, k, v, g, lr, wlr, blr, gn, s0t)


def _ssd_body(z_ref, xbc_ref, dtr_ref, c0_ref, e_ref, cw_ref, cb_ref, dtb_ref, alog_ref, d_ref, nw_ref, s0_ref,
              o_ref, so_ref, cn_ref, s_ref, xb_ref, *, T, C, valid, nt):
    t = pl.program_id(1)

    @pl.when(t == 0)
    def _():
        s_ref[...] = s0_ref[...]
        xb_ref[0:8, :] = c0_ref[...]

    xb_ref[8:8 + T, :] = xbc_ref[...]
    conv = cb_ref[...] + xb_ref[5:5 + T, :] * cw_ref[0:1, :]
    for j in range(1, SSD_CONV):
        conv = conv + xb_ref[5 + j:5 + j + T, :] * cw_ref[j:j + 1, :]
    act = _silu(conv)
    dt = _softplus(_nn(dtr_ref[...], e_ref[...], HI) + dtb_ref[...])
    v_last = valid - (nt - 1) * T
    if v_last < T:
        dt = jnp.where(_iota((T, 1), 0) < v_last, dt, 0.0)
    lac = dt * (-jnp.exp(alog_ref[...]))

    row = _iota((C, C), 0)
    col = _iota((C, C), 1)
    causal = col <= row
    triu = (row <= col).astype(F32)
    lane_x = _iota((1, 256), 1)
    shift = C.bit_length() - 1
    assert (1 << shift) == C
    trow = _iota((T, T), 0)
    tcol = _iota((T, T), 1)
    same_chunk_upto = jnp.where((tcol <= trow) & ((trow >> shift) == (tcol >> shift)), 1.0, 0.0)
    cum_all = _nn(same_chunk_upto, lac, HI)
    s_all = s_ref[...]
    for c in range(T // C):
        sl = slice(c * C, (c + 1) * C)
        x = act[sl, 0:256]
        bm = act[sl, 256:512]
        cm = act[sl, 512:768]
        xdt = x * dt[sl, 0:256]
        cum64 = cum_all[sl, 0:256]
        cum128 = cum_all[sl, 256:768]
        cum_t = _tn(lac[sl, 256:768], triu, HI)
        cb = [_nt(_b(cm[:, g * 128:(g + 1) * 128]), _b(bm[:, g * 128:(g + 1) * 128])) for g in range(2)]
        y = x * d_ref[...]
        for h in range(N_HEADS):
            seg = cum128[:, h * 128:h * 128 + C] - cum_t[h * 128:h * 128 + C, :]
            w = jnp.where(causal, cb[h // 2] * jnp.exp(jnp.where(causal, seg, 0.0)), 0.0)
            y = y + _nn(_b(w), _b(jnp.where((lane_x >> 6) == h, xdt, 0.0)))
        ys0 = _nt(_b(cm[:, 0:128]), _b(s_all))
        ys1 = _nt(_b(cm[:, 128:256]), _b(s_all))
        y = y + jnp.where(lane_x < 128, ys0, ys1) * jnp.exp(cum64)
        new_rows = []
        for h in range(N_HEADS):
            g = h // 2
            c_end = cum128[C - 1:C, h * 128:(h + 1) * 128]
            bw = bm[:, g * 128:(g + 1) * 128] * jnp.exp(c_end - cum128[:, h * 128:(h + 1) * 128])
            upd = _tn(_b(xdt[:, h * 64:(h + 1) * 64]), _b(bw))
            new_rows.append(s_all[h * 64:(h + 1) * 64, :] * jnp.exp(c_end) + upd)
        s_all = jnp.concatenate(new_rows, axis=0)
        o_ref[sl, :] = _rms(y * _silu(z_ref[sl, :]), nw_ref[...])
    s_ref[...] = s_all
    xb_ref[0:8, :] = xb_ref[T:T + 8, :]

    @pl.when(t == nt - 1)
    def _():
        so_ref[...] = s_all
        cn_ref[...] = jnp.zeros_like(cn_ref)
        cn_ref[0:SSD_CONV - 1, :] = xb_ref[8 + v_last - (SSD_CONV - 1):8 + v_last, :]


def _ssd(z, xbc, dtr, conv0, consts, s0, s0_index, *, B, L, valid, name):
    T = min(256, L)
    C = min(SSD_N, T)
    nt = L // T
    assert valid == L or nt == 1
    tok = lambda w: pl.BlockSpec((T, w), lambda b, t: (b * nt + t, 0))
    const = lambda a: pl.BlockSpec(a.shape, lambda b, t: (0,) * a.ndim)
    l_idx = s0_index
    s0_spec = pl.BlockSpec((None, None, 256, 128), lambda b, t: (l_idx, b, 0, 0))
    so_spec = pl.BlockSpec((None, 256, 128), lambda b, t: (b, 0, 0))
    cv_spec = pl.BlockSpec((None, 8, 768), lambda b, t: (b, 0, 0))
    return pl.pallas_call(
        functools.partial(_ssd_body, T=T, C=C, valid=valid, nt=nt),
        grid=(B, nt),
        in_specs=[tok(256), tok(768), tok(128), cv_spec] + [const(a) for a in consts] + [s0_spec],
        out_specs=[tok(256), so_spec, cv_spec],
        out_shape=[jax.ShapeDtypeStruct((B * L, 256), F32), jax.ShapeDtypeStruct((B, 256, 128), F32),
                   jax.ShapeDtypeStruct((B, 8, 768), F32)],
        scratch_shapes=[pltpu.VMEM((256, 128), F32), pltpu.VMEM((T + 8, 768), F32)],
        compiler_params=_params(2),
        name=name,
    )(z, xbc, dtr, conv0, *consts, s0)


def _head_cols(h):
    return (_iota((1, 256), 1) >> 6) == h


def _score_block(n, qh, kb_ref, bias_of_head, s_ref, m_ref):
    kb = kb_ref[n]
    for h in range(N_HEADS):
        s = _nn(qh[h], kb[h * 64:(h + 1) * 64, :]) + bias_of_head(h)
        s_ref[h, n] = s
        lo, hi = _fold(s)
        m_ref[h] = jnp.maximum(m_ref[h], jnp.maximum(lo, hi))


def _value_block(n, m_rows, vb_ref, s_ref, l_ref, acc_ref):
    vb = vb_ref[n]
    tot = None
    for h in range(N_HEADS):
        p = jnp.exp(s_ref[h, n] - m_rows[h])
        lo, hi = _fold(p)
        l_ref[h] = l_ref[h] + lo + hi
        pv = jnp.where(_head_cols(h), _nt(_b(p), vb), 0.0)
        tot = pv if tot is None else tot + pv
    acc_ref[...] = acc_ref[...] + tot


def _finish_heads(l_ref, acc_ref):
    inv = jnp.zeros(acc_ref.shape, F32)
    for h in range(N_HEADS):
        inv = inv + jnp.where(_head_cols(h), 1.0 / jnp.sum(l_ref[h], axis=-1, keepdims=True), 0.0)
    return acc_ref[...] * inv


def _moba_prompt_body(q_ref, kt_ref, vt_ref, g_ref, o_ref, kb_ref, vb_ref, km_ref, s_ref, m_ref, l_ref, acc_ref,
                      *, L, nb):
    i = pl.program_id(1)
    T = MOBA_BLOCK
    nbp = km_ref.shape[0]

    @pl.when(i == 0)
    def _():
        for n in range(nb):
            kb_ref[n] = _b(kt_ref[:, n * T:(n + 1) * T])
            vb_ref[n] = _b(vt_ref[:, n * T:(n + 1) * T])
        mean_of_block = jnp.where((_iota((nbp, L), 1) >> 8) == _iota((nbp, L), 0), 1.0 / T, 0.0)
        km_ref[...] = _nt(mean_of_block, kt_ref[...], HI)

    q = q_ref[...]
    km = km_ref[...]
    blk = _iota((nbp, T), 0)
    n_sel = min(MOBA_TOPK, nb)
    sel_cols = []
    for h in range(N_HEADS):
        gt = _nt(jnp.where(_head_cols(h), km, 0.0), q, HI)
        rank = jnp.zeros((nbp, T), jnp.int32)
        for n2 in range(nb):
            other = gt[n2:n2 + 1, :]
            beats = (other > gt) | ((other == gt) & (n2 < blk))
            rank = rank + jnp.where(beats, 1, 0) * jnp.where(n2 < i, 1, 0)
        sel_cols.append(jnp.where((blk < i) & (rank < n_sel), 1.0, 0.0).T)
    qh = [_b(q[:, h * 64:(h + 1) * 64]) for h in range(N_HEADS)]
    causal_bias = jnp.where(_iota((T, T), 1) <= _iota((T, T), 0), 0.0, NEG)

    m_ref[...] = jnp.full(m_ref.shape, NEG, F32)
    for n in range(nb - 1):
        @pl.when(n < i)
        def _():
            _score_block(n, qh, kb_ref, lambda h: jnp.where(sel_cols[h][:, n:n + 1] > 0.5, 0.0, NEG), s_ref, m_ref)
    _score_block(i, qh, kb_ref, lambda h: causal_bias, s_ref, m_ref)
    m_rows = [jnp.max(m_ref[h], axis=-1, keepdims=True) for h in range(N_HEADS)]

    l_ref[...] = jnp.zeros_like(l_ref)
    acc_ref[...] = jnp.zeros_like(acc_ref)
    for n in range(nb - 1):
        @pl.when(n < i)
        def _():
            _value_block(n, m_rows, vb_ref, s_ref, l_ref, acc_ref)
    _value_block(i, m_rows, vb_ref, s_ref, l_ref, acc_ref)
    o_ref[...] = _finish_heads(l_ref, acc_ref) * _silu(g_ref[...])


def _attn_scratch(T, nb):
    return [pltpu.VMEM((N_HEADS, nb, T, MOBA_BLOCK), F32), pltpu.VMEM((N_HEADS, T, LANES), F32),
            pltpu.VMEM((N_HEADS, T, LANES), F32), pltpu.VMEM((T, 256), F32)]


def _moba_prompt(q, kt, vt, g, *, B, L, name):
    T = MOBA_BLOCK
    nb = L // T
    nbp = -(-nb // 8) * 8
    tok = pl.BlockSpec((T, 256), lambda b, i: (b * nb + i, 0))
    seq = pl.BlockSpec((None, 256, L), lambda b, i: (b, 0, 0))
    return pl.pallas_call(
        functools.partial(_moba_prompt_body, L=L, nb=nb),
        grid=(B, nb),
        in_specs=[tok, seq, seq, tok],
        out_specs=tok,
        out_shape=jax.ShapeDtypeStruct((B * L, 256), F32),
        scratch_shapes=[pltpu.VMEM((nb, 256, T), BF16), pltpu.VMEM((nb, 256, T), BF16), pltpu.VMEM((nbp, 256), F32)]
        + _attn_scratch(T, nb),
        compiler_params=_params(2),
        name=name,
    )(q, kt, vt, g)


def _sortable_keys(score):
    bits = pltpu.bitcast(score, jnp.int32)
    keys = jnp.where(bits < 0, bits ^ jnp.int32(0x7FFFFFFF), bits)
    return jnp.where(score == 0.0, 0, keys)


def _sortable_neg_inf():
    return int(np.array(-np.inf, np.float32).view(np.int32)) ^ 0x7FFFFFFF


def _kth_largest(k, count_ge, zero):
    def body(i, ans):
        cand = ans | jnp.left_shift(jnp.int32(1), 31 - i)
        return jnp.where(count_ge(cand ^ jnp.int32(_INT_MIN)) >= k, cand, ans)

    return lax.fori_loop(0, 32, body, zero) ^ jnp.int32(_INT_MIN)


def _fold_rows(x, op):
    while x.shape[0] > SUBLANES:
        half = x.shape[0] // 2
        x = op(x[:half], x[half:])
    return x


def _dsa_prompt_body_t(q_ref, kt_ref, vt_ref, g_ref, qi_ref, wi_ref, kit_ref, o_ref,
                       krm_ref, vb_ref, kirm_ref, keys_ref, bias_ref, thr_ref, cge_ref, cgt_ref,
                       s_ref, m_ref, l_ref, acc_ref, *, L, nb, n_sel):
    i = pl.program_id(1)
    T = MOBA_BLOCK

    @pl.when(i == 0)
    def _():
        for n in range(nb):
            blk = slice(n * T, (n + 1) * T)
            vb_ref[n] = _b(vt_ref[:, blk])
            kirm_ref[n] = _b(kit_ref[:, blk].T)
            for h in range(N_HEADS):
                krm_ref[n, h] = _b(kt_ref[h * 64:(h + 1) * 64, blk].T)

    qi = qi_ref[...]
    qib = [_b(qi[:, h * 64:(h + 1) * 64]) for h in range(DSA_IDX_HEADS)]
    wit = wi_ref[...].T

    def index_block(n):
        kirm = kirm_ref[n]
        sc = wit[0:1, :] * jnp.maximum(_nt(kirm, qib[0]), 0.0)
        for h in range(1, DSA_IDX_HEADS):
            sc = sc + wit[h:h + 1, :] * jnp.maximum(_nt(kirm, qib[h]), 0.0)
        return sc

    for n in range(nb - 1):
        @pl.when(n < i)
        def _():
            keys_ref[n] = _sortable_keys(index_block(n))
    visible = _iota((T, T), 0) <= _iota((T, T), 1)
    keys_ref[i] = _sortable_keys(jnp.where(visible, index_block(i), -jnp.inf))
    for n in range(1, nb):
        @pl.when(n > i)
        def _():
            keys_ref[n] = jnp.full((T, T), _INT_MIN, jnp.int32)

    def threshold_over(n_blocks):
        def count(pred):
            part = jnp.zeros((SUBLANES, T), jnp.int32)
            for n in range(n_blocks):
                part = part + _fold_rows(jnp.where(pred(keys_ref[n]), 1, 0), jnp.add)
            return jnp.sum(part, axis=0, keepdims=True)

        t = _kth_largest(n_sel, lambda c: count(lambda kk: kk >= c), jnp.zeros((1, T), jnp.int32))
        thr_ref[...] = jnp.broadcast_to(t, thr_ref.shape)
        cge_ref[...] = jnp.broadcast_to(count(lambda kk: kk >= t), cge_ref.shape)
        cgt_ref[...] = jnp.broadcast_to(count(lambda kk: kk > t), cgt_ref.shape)

    prev = 0
    for n_blocks in sorted({max(1, -(-nb * quarter // 4)) for quarter in range(1, 5)}):
        @pl.when((i >= prev) & (i < n_blocks))
        def _():
            threshold_over(n_blocks)
        prev = n_blocks
    thr = thr_ref[0:1, :]
    cnt_ge = cge_ref[0:1, :]
    cnt_gt = cgt_ref[0:1, :]
    few = (i * T + _iota((1, T), 1) + 1) <= n_sel
    neg_inf_key = jnp.int32(_sortable_neg_inf())
    thr_eff = jnp.where(few, neg_inf_key + 1, thr)

    def write_bias(n, carry):
        bias_ref[n] = jnp.where(keys_ref[n] >= thr_eff, 0.0, NEG)
        return carry

    lax.fori_loop(0, i + 1, write_bias, 0)
    tied = jnp.where(few, 0.0, jnp.where(cnt_ge > n_sel, 1.0, 0.0))

    @pl.when(jnp.max(tied) > 0.5)
    def _():
        need = (n_sel - cnt_gt).astype(F32)
        upto_here = jnp.where(_iota((T, T), 1) <= _iota((T, T), 0), 1.0, 0.0).astype(BF16)

        def tie_block(n, seen):
            kc = keys_ref[n]
            eq = jnp.where(kc == thr, 1.0, 0.0)
            upto = _nn(upto_here, _b(eq)) + seen
            pick = jnp.where(kc > thr, 1.0, jnp.where(upto <= need, eq, 0.0))
            tie_bias = jnp.where(pick > 0.5, 0.0, NEG)
            bias_ref[n] = jnp.where(few, jnp.where(kc > neg_inf_key, 0.0, NEG), tie_bias)
            return seen + jnp.sum(eq, axis=0, keepdims=True)

        lax.fori_loop(0, i + 1, tie_block, jnp.zeros((1, T), F32))

    q = q_ref[...]
    qh = [_b(q[:, h * 64:(h + 1) * 64]) for h in range(N_HEADS)]
    m_ref[...] = jnp.full(m_ref.shape, NEG, F32)

    def pass1(n, carry):
        for h in range(N_HEADS):
            st = _nt(krm_ref[n, h], qh[h]) + bias_ref[n]
            s_ref[h, n] = st
            m_ref[h] = jnp.maximum(m_ref[h], _fold_rows(st, jnp.maximum))
        return carry

    lax.fori_loop(0, i + 1, pass1, 0)
    m_rows = [jnp.max(m_ref[h], axis=0, keepdims=True) for h in range(N_HEADS)]
    l_ref[...] = jnp.zeros_like(l_ref)
    acc_ref[...] = jnp.zeros_like(acc_ref)

    def pass2(n, carry):
        vb = vb_ref[n]
        for h in range(N_HEADS):
            hs = slice(h * 64, (h + 1) * 64)
            p = jnp.exp(s_ref[h, n] - m_rows[h])
            l_ref[h] = l_ref[h] + _fold_rows(p, jnp.add)
            acc_ref[hs, :] = acc_ref[hs, :] + _nn(vb[hs, :], _b(p))
        return carry

    lax.fori_loop(0, i + 1, pass2, 0)
    rows = []
    for h in range(N_HEADS):
        rows.append(acc_ref[h * 64:(h + 1) * 64, :] * (1.0 / jnp.sum(l_ref[h], axis=0, keepdims=True)))
    o_ref[...] = jnp.concatenate(rows, axis=0).T * _silu(g_ref[...])


def _dsa_prompt(q, kt, vt, g, qi, wi, kit, *, B, L, name):
    T = MOBA_BLOCK
    nb = L // T
    n_sel = max(1, min(DSA_TOPK, L // 4))
    tok = lambda w: pl.BlockSpec((T, w), lambda b, i: (b * nb + i, 0))
    seq = lambda w: pl.BlockSpec((None, w, L), lambda b, i: (b, 0, 0))
    return pl.pallas_call(
        functools.partial(_dsa_prompt_body_t, L=L, nb=nb, n_sel=n_sel),
        grid=(B, nb),
        in_specs=[tok(256), seq(256), seq(256), tok(256), tok(512), tok(128), seq(64)],
        out_specs=tok(256),
        out_shape=jax.ShapeDtypeStruct((B * L, 256), F32),
        scratch_shapes=[pltpu.VMEM((nb, N_HEADS, T, HEAD_DIM), BF16), pltpu.VMEM((nb, 256, T), BF16),
                        pltpu.VMEM((nb, T, HEAD_DIM), BF16),
                        pltpu.VMEM((nb, T, T), jnp.int32), pltpu.VMEM((nb, T, T), F32),
                        pltpu.VMEM((SUBLANES, T), jnp.int32), pltpu.VMEM((SUBLANES, T), jnp.int32),
                        pltpu.VMEM((SUBLANES, T), jnp.int32),
                        pltpu.VMEM((N_HEADS, nb, T, T), F32), pltpu.VMEM((N_HEADS, SUBLANES, T), F32),
                        pltpu.VMEM((N_HEADS, SUBLANES, T), F32), pltpu.VMEM((256, T), F32)],
        compiler_params=_params(2),
        name=name,
    )(q, kt, vt, g, qi, wi, kit)


def _stack_heads(q):
    return jnp.concatenate([jnp.where(_head_cols(h), q, 0.0) for h in range(N_HEADS)], axis=0)


def _unstack_heads(o):
    out = jnp.zeros((SAMPLE_PAD, 256), F32)
    for h in range(N_HEADS):
        out = out + jnp.where(_head_cols(h), o[h * SAMPLE_PAD:(h + 1) * SAMPLE_PAD, :], 0.0)
    return out


def _page_scores(qs, pages):
    return jnp.concatenate([_nn(qs, _b(p[...])) for p in pages], axis=1)


def _page_values(p, pages, acc):
    for r, page in enumerate(pages):
        acc = acc + _nt(_b(p[:, r * LANES:(r + 1) * LANES]), _b(page[...]))
    return acc


def _own_scores(qs, kn_ref):
    R = qs.shape[0]
    s = _nt(qs, _b(kn_ref[...]))
    qrow = _iota((R, SAMPLE_PAD), 0) & (SAMPLE_PAD - 1)
    return jnp.where(_iota((R, SAMPLE_PAD), 1) <= qrow, s, NEG)


def _softmax_outputs(s_own, s_ref, p_ref, l_ref, pown_ref, n_ch):
    R = s_own.shape[0]
    m_all = jnp.max(s_own, axis=-1, keepdims=True)
    for c in range(n_ch):
        m_all = jnp.maximum(m_all, jnp.max(s_ref[c], axis=-1, keepdims=True))
    p_own = jnp.exp(s_own - m_all)
    l = jnp.sum(p_own, axis=-1, keepdims=True)
    for c in range(n_ch):
        p = jnp.exp(s_ref[c] - m_all)
        p_ref[c] = p
        l = l + jnp.sum(p, axis=-1, keepdims=True)
    l_ref[...] = jnp.broadcast_to(l, l_ref.shape)
    pown_ref[...] = jnp.concatenate([p_own, jnp.zeros((R, LANES - SAMPLE_PAD), F32)], axis=1)


def _moba_sample_body(pt_ref, *refs, pps, n_ch, nblk):
    kp = refs[0:pps]
    q_ref, kn_ref, p_ref, l_ref, pown_ref, qs_ref, gate_ref, s_ref = refs[pps:]
    j = pl.program_id(1)
    R = N_HEADS * SAMPLE_PAD
    W = pps * LANES
    lane = _iota((R, LANES), 1)

    @pl.when(j == 0)
    def _():
        qs_ref[...] = _b(_stack_heads(q_ref[...]))
        gate_ref[...] = jnp.zeros_like(gate_ref)

    s = _page_scores(qs_ref[...], kp)
    s_ref[j] = s
    for r in range(W // MOBA_BLOCK):
        n = j * (W // MOBA_BLOCK) + r
        g = jnp.sum(s[:, r * MOBA_BLOCK:(r + 1) * MOBA_BLOCK], axis=-1, keepdims=True) * (1.0 / MOBA_BLOCK)
        gate_ref[...] = jnp.where((lane & (nblk - 1)) == n, g, gate_ref[...])

    @pl.when(j == n_ch - 1)
    def _():
        gate = gate_ref[...]
        rank = jnp.zeros((R, LANES), jnp.int32)
        for d in range(1, nblk):
            other = pltpu.roll(gate, LANES - d, 1)
            wraps = (lane & (nblk - 1)) + d >= nblk
            rank = rank + jnp.where((other > gate) | ((other == gate) & wraps), 1, 0)
        sel = jnp.where((rank < min(MOBA_TOPK, nblk + 1)) & (lane < nblk), 1.0, 0.0).astype(BF16)
        for c in range(n_ch):
            expand = jnp.where(((c * W + _iota((LANES, W), 1)) >> 8) == _iota((LANES, W), 0), 1.0, 0.0).astype(BF16)
            s_ref[c] = jnp.where(_nn(sel, expand) > 0.5, s_ref[c], NEG)
        _softmax_outputs(_own_scores(qs_ref[...], kn_ref), s_ref, p_ref, l_ref, pown_ref, n_ch)


def _sample_values_body(pt_ref, *refs, pps, n_ch):
    vp = refs[0:pps]
    p_ref, l_ref, pown_ref, vn_ref, g_ref, o_ref, acc_ref = refs[pps:]
    j = pl.program_id(1)

    @pl.when(j == 0)
    def _():
        acc_ref[...] = _nn(_b(pown_ref[:, 0:SAMPLE_PAD]), _b(vn_ref[...]))

    acc_ref[...] = _page_values(p_ref[...], vp, acc_ref[...])

    @pl.when(j == n_ch - 1)
    def _():
        o = acc_ref[...] * (1.0 / l_ref[:, 0:1])
        o_ref[...] = _unstack_heads(o) * _silu(g_ref[...])


def _page_specs(layer, pps, rows):
    def spec(r):
        return pl.BlockSpec((None, None, rows, LANES), lambda b, j, pt: (layer, pt[b, j * pps + r], 0, 0))
    return [spec(r) for r in range(pps)]


def _sample_chunks(page_table):
    n_pages = page_table.shape[1]
    pps = min(SAMPLE_PAGES_PER_STEP, n_pages)
    n_ch = n_pages // pps
    assert n_ch * pps == n_pages and pps % 2 == 0
    return pps, n_ch


def _prob_shapes(B, n_ch, W):
    R = N_HEADS * SAMPLE_PAD
    shapes = [jax.ShapeDtypeStruct((B, n_ch, R, W), F32), jax.ShapeDtypeStruct((B, R, LANES), F32),
              jax.ShapeDtypeStruct((B, R, LANES), F32)]
    specs = [pl.BlockSpec((None, n_ch, R, W), lambda b, j, pt: (b, 0, 0, 0)),
             pl.BlockSpec((None, R, LANES), lambda b, j, pt: (b, 0, 0)),
             pl.BlockSpec((None, R, LANES), lambda b, j, pt: (b, 0, 0))]
    return shapes, specs


def _sample_values(page_table, vt_pool, layer, p, l, pown, vn, g, *, B, name):
    pps, n_ch = _sample_chunks(page_table)
    R = N_HEADS * SAMPLE_PAD
    W = pps * LANES
    tok = pl.BlockSpec((SAMPLE_PAD, 256), lambda b, j, pt: (b, 0))
    row = pl.BlockSpec((None, R, LANES), lambda b, j, pt: (b, 0, 0))
    grid_spec = pltpu.PrefetchScalarGridSpec(
        num_scalar_prefetch=1,
        grid=(B, n_ch),
        in_specs=_page_specs(layer, pps, 256)
        + [pl.BlockSpec((None, None, R, W), lambda b, j, pt: (b, j, 0, 0)), row, row, tok, tok],
        out_specs=tok,
        scratch_shapes=[pltpu.VMEM((R, 256), F32)],
    )
    return pl.pallas_call(
        functools.partial(_sample_values_body, pps=pps, n_ch=n_ch),
        grid_spec=grid_spec,
        out_shape=jax.ShapeDtypeStruct((B * SAMPLE_PAD, 256), F32),
        compiler_params=_params(2),
        name=name,
    )(page_table, *([vt_pool] * pps), p, l, pown, vn, g)


def _moba_sample(page_table, kt_pool, vt_pool, layer, q, kn, vn, g, *, B, name):
    n_pages = page_table.shape[1]
    nblk = n_pages // 2
    assert nblk * 2 == n_pages and LANES % nblk == 0
    pps, n_ch = _sample_chunks(page_table)
    R = N_HEADS * SAMPLE_PAD
    W = pps * LANES
    tok = pl.BlockSpec((SAMPLE_PAD, 256), lambda b, j, pt: (b, 0))
    out_shapes, out_specs = _prob_shapes(B, n_ch, W)
    grid_spec = pltpu.PrefetchScalarGridSpec(
        num_scalar_prefetch=1,
        grid=(B, n_ch),
        in_specs=_page_specs(layer, pps, 256) + [tok, tok],
        out_specs=out_specs,
        scratch_shapes=[pltpu.VMEM((R, 256), BF16), pltpu.VMEM((R, LANES), F32), pltpu.VMEM((n_ch, R, W), F32)],
    )
    p, l, pown = pl.pallas_call(
        functools.partial(_moba_sample_body, pps=pps, n_ch=n_ch, nblk=nblk),
        grid_spec=grid_spec,
        out_shape=out_shapes,
        compiler_params=_params(2),
        name=name + "_scores",
    )(page_table, *([kt_pool] * pps), q, kn)
    return _sample_values(page_table, vt_pool, layer, p, l, pown, vn, g, B=B, name=name + "_values")


def _dsa_sample_body(pt_ref, *refs, pps, n_ch, n_sel, past):
    ikp = refs[0:pps]
    kp = refs[pps:2 * pps]
    (q_ref, kn_ref, qi_ref, wi_ref, kin_ref, p_ref, l_ref, pown_ref,
     qs_ref, qis_ref, wis_ref, sc_ref, scn_ref, bias_ref, biasn_ref, s_ref) = refs[2 * pps:]
    j = pl.program_id(1)
    R = N_HEADS * SAMPLE_PAD
    RI = DSA_IDX_HEADS * SAMPLE_PAD
    W = pps * LANES

    def index_scores(a):
        a = jnp.maximum(a, 0.0) * wis_ref[:, 0:1]
        out = a[0:SAMPLE_PAD, :]
        for h in range(1, DSA_IDX_HEADS):
            out = out + a[h * SAMPLE_PAD:(h + 1) * SAMPLE_PAD, :]
        return out

    @pl.when(j == 0)
    def _():
        qs_ref[...] = _b(_stack_heads(q_ref[...]))
        qi = qi_ref[...]
        wi = wi_ref[...]
        qis_ref[...] = _b(jnp.concatenate([qi[:, h * 64:(h + 1) * 64] for h in range(DSA_IDX_HEADS)], axis=0))
        wcol = jnp.concatenate([wi[:, h:h + 1] for h in range(DSA_IDX_HEADS)], axis=0)
        wis_ref[...] = jnp.broadcast_to(wcol, (RI, LANES))

    sc_ref[j] = index_scores(_page_scores(qis_ref[...], ikp))
    s_ref[j] = _page_scores(qs_ref[...], kp)

    @pl.when(j == n_ch - 1)
    def _():
        s_new = index_scores(_nt(qis_ref[...], _b(kin_ref[...])))
        ok = _iota((SAMPLE_PAD, SAMPLE_PAD), 1) <= _iota((SAMPLE_PAD, SAMPLE_PAD), 0)
        s_new = jnp.where(ok, s_new, -jnp.inf)
        scn_ref[...] = jnp.concatenate([s_new, jnp.full((SAMPLE_PAD, LANES - SAMPLE_PAD), -jnp.inf, F32)], axis=1)
        keys = _sortable_keys(sc_ref[...])
        keys_new = _sortable_keys(scn_ref[...])

        def count(pred):
            past_cnt = jnp.sum(jnp.sum(jnp.where(pred(keys), 1, 0), axis=0), axis=-1, keepdims=True)
            return past_cnt + jnp.sum(jnp.where(pred(keys_new), 1, 0), axis=-1, keepdims=True)

        thr = _kth_largest(n_sel, lambda t: count(lambda kk: kk >= t), jnp.zeros((SAMPLE_PAD, 1), jnp.int32))
        cnt_ge = count(lambda kk: kk >= thr)
        cnt_gt = count(lambda kk: kk > thr)
        few = (past + _iota((SAMPLE_PAD, 1), 0) + 1) <= n_sel
        neg_inf_key = jnp.int32(_sortable_neg_inf())
        thr_eff = jnp.where(few, neg_inf_key + 1, thr)
        bias_ref[...] = jnp.where(keys >= thr_eff, 0.0, NEG)
        biasn_ref[...] = jnp.where(keys_new >= thr_eff, 0.0, NEG)
        tied = jnp.where(few, 0.0, jnp.where(cnt_ge > n_sel, 1.0, 0.0))

        @pl.when(jnp.max(tied) > 0.5)
        def _():
            need = (n_sel - cnt_gt).astype(F32)
            tri = jnp.where(_iota((LANES, LANES), 0) <= _iota((LANES, LANES), 1), 1.0, 0.0).astype(BF16)

            def tie_lanes(kc, seen):
                eq = jnp.where(kc == thr, 1.0, 0.0)
                upto = _nn(_b(eq), tri) + seen
                pick = jnp.where(kc > thr, 1.0, jnp.where(upto <= need, eq, 0.0))
                tie_bias = jnp.where(pick > 0.5, 0.0, NEG)
                bias = jnp.where(few, jnp.where(kc > neg_inf_key, 0.0, NEG), tie_bias)
                return bias, seen + jnp.sum(eq, axis=-1, keepdims=True)

            def tie_chunk(c, seen):
                kch = _sortable_keys(sc_ref[c])
                for r in range(pps):
                    cs = slice(r * LANES, (r + 1) * LANES)
                    bias, seen = tie_lanes(kch[:, cs], seen)
                    bias_ref[c, :, cs] = bias
                return seen

            seen = lax.fori_loop(0, n_ch, tie_chunk, jnp.zeros((SAMPLE_PAD, 1), F32))
            bias, _ = tie_lanes(keys_new, seen)
            biasn_ref[...] = bias

        for c in range(n_ch):
            s_ref[c] = s_ref[c] + jnp.concatenate([bias_ref[c]] * N_HEADS, axis=0)
        s_own = _own_scores(qs_ref[...], kn_ref) + jnp.concatenate([biasn_ref[:, 0:SAMPLE_PAD]] * N_HEADS, axis=0)
        _softmax_outputs(s_own, s_ref, p_ref, l_ref, pown_ref, n_ch)


def _dsa_sample(page_table, kt_pool, vt_pool, ikt_pool, layer, q, kn, vn, g, qi, wi, kin, *, B, valid, name):
    n_pages = page_table.shape[1]
    past = n_pages * LANES
    pps, n_ch = _sample_chunks(page_table)
    n_sel = max(1, min(DSA_TOPK, (past + valid) // 4))
    R = N_HEADS * SAMPLE_PAD
    RI = DSA_IDX_HEADS * SAMPLE_PAD
    W = pps * LANES
    tok = lambda w: pl.BlockSpec((SAMPLE_PAD, w), lambda b, j, pt: (b, 0))
    out_shapes, out_specs = _prob_shapes(B, n_ch, W)
    grid_spec = pltpu.PrefetchScalarGridSpec(
        num_scalar_prefetch=1,
        grid=(B, n_ch),
        in_specs=_page_specs(layer, pps, 64) + _page_specs(layer, pps, 256)
        + [tok(256), tok(256), tok(512), tok(128), tok(64)],
        out_specs=out_specs,
        scratch_shapes=[pltpu.VMEM((R, 256), BF16), pltpu.VMEM((RI, 64), BF16), pltpu.VMEM((RI, LANES), F32),
                        pltpu.VMEM((n_ch, SAMPLE_PAD, W), F32), pltpu.VMEM((SAMPLE_PAD, LANES), F32),
                        pltpu.VMEM((n_ch, SAMPLE_PAD, W), F32), pltpu.VMEM((SAMPLE_PAD, LANES), F32),
                        pltpu.VMEM((n_ch, R, W), F32)],
    )
    p, l, pown = pl.pallas_call(
        functools.partial(_dsa_sample_body, pps=pps, n_ch=n_ch, n_sel=n_sel, past=past),
        grid_spec=grid_spec,
        out_shape=out_shapes,
        compiler_params=_params(2),
        name=name + "_scores",
    )(page_table, *([ikt_pool] * pps), *([kt_pool] * pps), q, kn, qi, wi, kin)
    return _sample_values(page_table, vt_pool, layer, p, l, pown, vn, g, B=B, name=name + "_values")


def _out_proj_body(a_ref, b_ref, c_ref, d_ref, x_ref, wo_ref, g1_ref, g2_ref, wq_ref, x1_ref, q_ref):
    acc = _nn(_b(a_ref[...]), wo_ref[0:256, :])
    acc = acc + _nn(_b(b_ref[...]), wo_ref[256:512, :])
    acc = acc + _nn(_b(c_ref[...]), wo_ref[512:768, :])
    acc = acc + _nn(_b(d_ref[...]), wo_ref[768:1024, :])
    x1 = x_ref[...] + _rms(acc, g1_ref[...])
    x1_ref[...] = x1
    q_ref[...] = _nn(_b(_rms(x1, g2_ref[...])), wq_ref[...]) * (XA_DIM ** -0.5)


def _out_proj(parts, x, wo, g1, g2, wq, *, name):
    n, d = x.shape
    tm = min(256, n)
    tok = lambda w: pl.BlockSpec((tm, w), lambda i: (i, 0))
    const = lambda a: pl.BlockSpec(a.shape, lambda i: (0,) * a.ndim)
    return pl.pallas_call(
        _out_proj_body,
        grid=(n // tm,),
        in_specs=[tok(256)] * 4 + [tok(d), const(wo), const(g1), const(g2), const(wq)],
        out_specs=[tok(d), tok(d)],
        out_shape=[jax.ShapeDtypeStruct((n, d), F32)] * 2,
        compiler_params=_params(1),
        name=name,
    )(*parts, x, wo, g1, g2, wq)


def _xattn_body(q_ref, mk_ref, mv_ref, wo_ref, g_ref, x_ref, o_ref):
    head_major = len(mk_ref.shape) == 3
    acc = jnp.zeros(x_ref.shape, F32)
    for h in range(N_HEADS):
        hs = slice(h * XA_DIM, (h + 1) * XA_DIM)
        mk = mk_ref[:, h, :] if head_major else mk_ref[:, hs]
        mv = mv_ref[:, h, :] if head_major else mv_ref[:, hs]
        s = _nt(_b(q_ref[:, hs]), _b(mk))
        m = jnp.max(s, axis=-1, keepdims=True)
        p = jnp.exp(s - m)
        oh = _nn(_b(p), _b(mv)) * (1.0 / jnp.sum(p, axis=-1, keepdims=True))
        acc = acc + _nn(_b(oh), wo_ref[hs, :])
    o_ref[...] = x_ref[...] + _rms(acc, g_ref[...])


def _xattn(q, mem_k, mem_v, layer, wo, g, x, *, B, L, name):
    T = min(256, L)
    nt = L // T
    d = x.shape[1]
    m = mem_k.shape[2]
    tok = pl.BlockSpec((T, d), lambda b, t: (b * nt + t, 0))
    if mem_k.ndim == 5:
        mem = pl.BlockSpec((None, None, m, N_HEADS, XA_DIM), lambda b, t: (layer, b, 0, 0, 0))
    else:
        mem = pl.BlockSpec((None, None, m, d), lambda b, t: (layer, b, 0, 0))
    const = lambda a: pl.BlockSpec(a.shape, lambda b, t: (0,) * a.ndim)
    return pl.pallas_call(
        _xattn_body,
        grid=(B, nt),
        in_specs=[tok, mem, mem, const(wo), const(g), tok],
        out_specs=tok,
        out_shape=jax.ShapeDtypeStruct(x.shape, F32),
        compiler_params=_params(2),
        name=name,
    )(q, mem_k, mem_v, wo, g, x)


def _pad_cols(w, width):
    return jnp.pad(w, ((0, 0), (0, width - w.shape[1])))


_IN_SEGS = (
    ("g_q", 0, 128, False, GLA_DK ** -0.5, 128), ("g_k", 1, 128, False, 1.0, 128), ("g_v", 2, 256, False, 1.0, 256),
    ("g_gate", 3, 256, False, 1.0, 256), ("g_lr", 4, 128, False, 1.0, 128),
    ("s_z", 5, 256, False, 1.0, 256), ("s_xbc", 6, 768, False, 1.0, 768), ("s_dt", 7, 128, False, 1.0, 128),
    ("m_q", 8, 256, True, HEAD_DIM ** -0.5, 256), ("m_k", 9, 256, True, 1.0, 256), ("m_v", 10, 256, False, 1.0, 256),
    ("m_gate", 11, 256, False, 1.0, 256),
    ("d_q", 12, 256, True, HEAD_DIM ** -0.5, 256), ("d_k", 13, 256, True, 1.0, 256), ("d_v", 14, 256, False, 1.0, 256),
    ("d_gate", 15, 256, False, 1.0, 256), ("d_qi", 16, 512, True, HEAD_DIM ** -0.5, 512),
    ("d_ki", 17, 128, True, 1.0, 64), ("d_wi", 18, 128, False, DSA_IDX_HEADS ** -0.5, 128),
)
_KV_SEGS = (("m_k", 9, 256, True), ("m_v", 10, 256, False), ("d_k", 13, 256, True), ("d_v", 14, 256, False),
            ("d_ki", 17, 64, True))


def _in_proj_layout(transposed_kv):
    offs = [int(o) for o in np.concatenate([[0], np.cumsum(IN_SPLITS)])]
    kv_names = {s[0] for s in _KV_SEGS} if transposed_kv else set()
    segs, names = [], []
    for name, idx, _, rope, scale, ow in _IN_SEGS:
        if name not in kv_names:
            segs.append((offs[idx], IN_SPLITS[idx], rope, scale, ow))
            names.append(name)
    tsegs, tnames = [], []
    if transposed_kv:
        for name, idx, width, rope in _KV_SEGS:
            tsegs.append((offs[idx], width, rope))
            tnames.append(name)
    return tuple(segs), names, tuple(tsegs), tnames


def _rope_tables(pos, rows):
    half = HEAD_DIM // 2
    inv = ROPE_THETA ** (-jnp.arange(half, dtype=F32) / half)
    ang = pos.astype(F32)[:, None] * inv[None, :]
    cos, sin = jnp.cos(ang), jnp.sin(ang)
    cs1 = jnp.concatenate([cos, cos], axis=1)
    sn1 = jnp.concatenate([-sin, sin], axis=1)
    cs = jnp.tile(cs1, (1, 512 // HEAD_DIM))
    sn = jnp.tile(sn1, (1, 512 // HEAD_DIM))
    if cs.shape[0] < rows:
        cs = jnp.tile(cs, (rows // cs.shape[0], 1))
        sn = jnp.tile(sn, (rows // sn.shape[0], 1))
    cst = jnp.tile(cs1.T, (256 // HEAD_DIM, 1))
    snt = jnp.tile(sn1.T, (256 // HEAD_DIM, 1))
    return (cs, sn), (cst, snt)


def _head_rep(v4, widths):
    return jnp.concatenate([jnp.repeat(v4, w) for w in widths]).reshape(1, -1)


def _layer_consts(l, W):
    e = np.zeros((128, 768), np.float32)
    for h in range(N_HEADS):
        e[h, h * 64:(h + 1) * 64] = 1.0
        e[h, 256 + h * 128:256 + (h + 1) * 128] = 1.0
    ssd_consts = (jnp.asarray(e),
                  jnp.pad(W["ssd_conv_w"][l], ((0, 8 - SSD_CONV), (0, 0))),
                  W["ssd_conv_b"][l].reshape(1, -1),
                  _head_rep(W["ssd_dt_bias"][l], (64, 128)),
                  _head_rep(W["ssd_a_log"][l], (64, 128)),
                  _head_rep(W["ssd_d"][l], (64,)),
                  W["ssd_norm"][l].reshape(1, -1))
    return dict(
        w_in=_b(_pad_cols(W["w_in"][l], -(-(sum(IN_SPLITS) + LANES) // LANES) * LANES)),
        wlr=jnp.pad(W["gla_w_lr"][l], ((0, 128 - GLA_RANK), (0, 0))),
        blr=W["gla_b_lr"][l].reshape(1, -1),
        gn=jnp.tile(W["gla_norm"][l], N_HEADS).reshape(1, -1),
        ssd=ssd_consts,
        wo=_b(W["w_out"][l]), wq=_b(W["xa_wq"][l]), xwo=_b(W["xa_wo"][l]),
        wkv=_b(jnp.concatenate([W["xa_wk"][l], W["xa_wv"][l]], axis=1)),
    )


def _gla_state_to_t(s):
    B = s.shape[0]
    same_head = jnp.arange(N_HEADS)[:, None, None, None] == jnp.arange(N_HEADS)[None, None, :, None]
    out = jnp.where(same_head, jnp.swapaxes(s, 2, 3)[:, :, :, None, :], 0.0)
    return out.reshape(B, 256, 128)


def _gla_state_from_t(st):
    B = st.shape[0]
    s5 = st.reshape(B, N_HEADS, 64, N_HEADS, GLA_DK)
    return jnp.stack([jnp.swapaxes(s5[:, h, :, h, :], 1, 2) for h in range(N_HEADS)], axis=1)


def _mixer_layer(x, l, C, W, tabs, *, B, L, valid, gla_s0t, ssd_s0, ssd_s0_idx, conv0, mem_k, mem_v, mem_idx,
                 sample_ctx, tag):
    rope, rope_t = tabs
    segs, names, tsegs, tnames = _in_proj_layout(sample_ctx is None)
    outs = _norm_proj(x, W["norm_mix_pre"][l], C["w_in"], segs, rope, tsegs, rope_t, L, name=f"in_proj_{tag}")
    P = dict(zip(names + tnames, outs))
    o_gla, gla_st = _gla(P["g_q"], P["g_k"], P["g_v"], P["g_gate"], P["g_lr"], C["wlr"], C["blr"], C["gn"], gla_s0t,
                         B=B, L=L, valid=valid, name=f"gla_{tag}")
    o_ssd, ssd_s, conv_new = _ssd(P["s_z"], P["s_xbc"], P["s_dt"], conv0, C["ssd"], ssd_s0, ssd_s0_idx,
                                  B=B, L=L, valid=valid, name=f"ssd_{tag}")
    if sample_ctx is None:
        o_moba = _moba_prompt(P["m_q"], P["m_k"], P["m_v"], P["m_gate"], B=B, L=L, name=f"moba_{tag}")
        o_dsa = _dsa_prompt(P["d_q"], P["d_k"], P["d_v"], P["d_gate"], P["d_qi"], P["d_wi"], P["d_ki"],
                            B=B, L=L, name=f"dsa_{tag}")
    else:
        pt, mk_pool, mv_pool, dk_pool, dv_pool, ik_pool = sample_ctx
        o_moba = _moba_sample(pt, mk_pool, mv_pool, l, P["m_q"], P["m_k"], P["m_v"], P["m_gate"], B=B,
                              name=f"moba_{tag}")
        o_dsa = _dsa_sample(pt, dk_pool, dv_pool, ik_pool, l, P["d_q"], P["d_k"], P["d_v"], P["d_gate"],
                            P["d_qi"], P["d_wi"], P["d_ki"], B=B, valid=valid, name=f"dsa_{tag}")
    x1, xq = _out_proj((o_gla, o_ssd, o_moba, o_dsa), x, C["wo"], W["norm_mix_post"][l].reshape(1, -1),
                       W["norm_xa_pre"][l].reshape(1, -1), C["wq"], name=f"out_proj_{tag}")
    x2 = _xattn(xq, mem_k, mem_v, mem_idx, C["xwo"], W["norm_xa_post"][l].reshape(1, -1), x1, B=B, L=L,
                name=f"xattn_{tag}")
    new = (gla_st, ssd_s, conv_new, P["m_k"], P["m_v"], P["d_k"], P["d_v"], P["d_ki"])
    return x2, new


def kernel(x_prompt, x_sample, state_gla, state_ssd, state_conv, cache_moba_k, cache_moba_v,
           cache_dsa_k, cache_dsa_v, cache_dsa_ik, cache_mem_k, cache_mem_v, page_table, mem_prompt,
           norm_mix_pre, norm_mix_post, norm_xa_pre, norm_xa_post, norm_mem, w_in, gla_w_lr, gla_b_lr,
           gla_norm, ssd_conv_w, ssd_conv_b, ssd_dt_bias, ssd_a_log, ssd_d, ssd_norm, w_out,
           xa_wq, xa_wk, xa_wv, xa_wo):
    W = dict(norm_mix_pre=norm_mix_pre, norm_mix_post=norm_mix_post, norm_xa_pre=norm_xa_pre,
             norm_xa_post=norm_xa_post, norm_mem=norm_mem, w_in=w_in, gla_w_lr=gla_w_lr, gla_b_lr=gla_b_lr,
             gla_norm=gla_norm, ssd_conv_w=ssd_conv_w, ssd_conv_b=ssd_conv_b, ssd_dt_bias=ssd_dt_bias,
             ssd_a_log=ssd_a_log, ssd_d=ssd_d, ssd_norm=ssd_norm, w_out=w_out, xa_wq=xa_wq, xa_wk=xa_wk,
             xa_wv=xa_wv, xa_wo=xa_wo)
    depth = w_in.shape[0]
    Bp, S, D = x_prompt.shape
    Bs, Ls, _ = x_sample.shape
    M = mem_prompt.shape[1]
    n_pool, page = cache_moba_k.shape[1], cache_moba_k.shape[2]
    assert page == LANES and Ls <= SAMPLE_PAD and S % MOBA_BLOCK == 0
    past = page_table.shape[1] * page
    consts = [_layer_consts(l, W) for l in range(depth)]

    tabs_p = _rope_tables(jnp.arange(S, dtype=jnp.int32), min(256, Bp * S))
    zero_gla = jnp.zeros((Bp, 256, 128), F32)
    zero_ssd = jnp.zeros((1, Bp, 256, 128), F32)
    zero_conv = jnp.zeros((Bp, 8, 768), F32)
    mem_flat = mem_prompt.reshape(Bp * M, D)
    kv_segs = ((0, D, False, 1.0, D), (D, D, False, 1.0, D))
    x = x_prompt.reshape(Bp * S, D)
    p_new = []
    for l in range(depth):
        C = consts[l]
        mk, mv = _norm_proj(mem_flat, norm_mem[l], C["wkv"], kv_segs, name="mem_kv")
        x, new = _mixer_layer(x, l, C, W, tabs_p, B=Bp, L=S, valid=S, gla_s0t=zero_gla, ssd_s0=zero_ssd,
                              ssd_s0_idx=0, conv0=zero_conv, mem_k=mk.reshape(1, Bp, M, D),
                              mem_v=mv.reshape(1, Bp, M, D), mem_idx=0, sample_ctx=None, tag="p")
        p_new.append(new + (mk.reshape(Bp, M, N_HEADS, XA_DIM), mv.reshape(Bp, M, N_HEADS, XA_DIM)))
    y_prompt = x.reshape(Bp, S, D)
    (p_gla, p_ssd, p_conv, p_mk, p_mv, p_dk, p_dv, p_ik, p_memk, p_memv) = [jnp.stack(t) for t in zip(*p_new)]
    p_gla = _gla_state_from_t(p_gla.reshape(depth * Bp, 256, 128)).reshape(depth, Bp, N_HEADS, GLA_DK, 64)
    p_ssd = p_ssd.reshape(depth, Bp, N_HEADS, 64, SSD_N)
    p_conv = p_conv[:, :, :SSD_CONV - 1]
    kv5 = lambda a: jnp.transpose(a.reshape(depth, Bp, N_HEADS, HEAD_DIM, S), (0, 1, 4, 2, 3))
    p_ik = jnp.transpose(p_ik, (0, 1, 3, 2))

    Lp = SAMPLE_PAD
    tabs_s = _rope_tables(past + jnp.arange(Lp, dtype=jnp.int32), min(256, Bs * Lp))
    xs = jnp.pad(x_sample, ((0, 0), (0, Lp - Ls), (0, 0))).reshape(Bs * Lp, D)
    gla_s0_all = _gla_state_to_t(state_gla.reshape(depth * Bs, N_HEADS, GLA_DK, 64)).reshape(depth, Bs, 256, 128)
    ssd_s0_all = state_ssd.reshape(depth, Bs, 256, SSD_N)
    conv0_all = jnp.pad(state_conv, ((0, 0), (0, 0), (8 - (SSD_CONV - 1), 0), (0, 0)))
    kv_t = lambda a: jnp.transpose(a, (0, 1, 3, 4, 2)).reshape(depth, n_pool, N_HEADS * HEAD_DIM, page)
    pools = (kv_t(cache_moba_k), kv_t(cache_moba_v), kv_t(cache_dsa_k), kv_t(cache_dsa_v),
             jnp.transpose(cache_dsa_ik, (0, 1, 3, 2)))
    s_new = []
    for l in range(depth):
        xs, new = _mixer_layer(xs, l, consts[l], W, tabs_s, B=Bs, L=Lp, valid=Ls, gla_s0t=gla_s0_all[l],
                               ssd_s0=ssd_s0_all, ssd_s0_idx=l, conv0=conv0_all[l], mem_k=cache_mem_k,
                               mem_v=cache_mem_v, mem_idx=l, sample_ctx=(page_table,) + pools, tag="s")
        s_new.append(new)
    y_sample = xs.reshape(Bs, Lp, D)[:, :Ls]
    (s_gla, s_ssd, s_conv, s_mk, s_mv, s_dk, s_dv, s_ik) = [jnp.stack(t) for t in zip(*s_new)]
    s_gla = _gla_state_from_t(s_gla.reshape(depth * Bs, 256, 128)).reshape(depth, Bs, N_HEADS, GLA_DK, 64)
    s_ssd = s_ssd.reshape(depth, Bs, N_HEADS, 64, SSD_N)
    s_conv = s_conv[:, :, :SSD_CONV - 1]
    skv5 = lambda a: a.reshape(depth, Bs, Lp, N_HEADS, HEAD_DIM)[:, :, :Ls]

    return (y_prompt, y_sample, p_gla, p_ssd, p_conv, kv5(p_mk), kv5(p_mv), kv5(p_dk), kv5(p_dv),
            p_ik, p_memk, p_memv,
            s_gla, s_ssd, s_conv, skv5(s_mk), skv5(s_mv), skv5(s_dk), skv5(s_dv),
            s_ik.reshape(depth, Bs, Lp, HEAD_DIM)[:, :, :Ls])
```
